```python
import jax, jax.numpy as jnp
from jax import lax
import numpy as np

D_MODEL = 2048
BATCH = 16
SEQ = 2048
DEPTH = 1
DEC_BATCH = 128
DEC_SEQ = 4
PAST_LEN = 16384
PAGE_SIZE = 128

HEAD_DIM = 64
ATTN_W = D_MODEL // 2
N_HEADS = ATTN_W // HEAD_DIM
N_KV_HEADS = N_HEADS // 4
GROUP = N_HEADS // N_KV_HEADS
KV_W = N_KV_HEADS * HEAD_DIM
CONV_CH = D_MODEL - ATTN_W
MIX_W = ATTN_W + CONV_CH
IN_W = ATTN_W + 2 * KV_W + 2 * CONV_CH
WINDOW = 128
BLOCK = 128
CACHE_WIN = min(WINDOW, PAST_LEN)
CONV_W = 31
CONV_BUF = min(CONV_W - 1, PAST_LEN)
D_FF = 4 * D_MODEL
EPS = 1e-6
ATTN_SCALE = HEAD_DIM ** -0.5
NEG = -1e30

kernel_name = "hymba_swa_sink_conformer_conv_sqrelu"


def _rmsnorm(x, g):
    xf = x.astype(jnp.float32)
    y = xf * lax.rsqrt(jnp.mean(xf * xf, axis=-1, keepdims=True) + EPS)
    return (y * g.astype(jnp.float32)).astype(x.dtype)


def _layernorm(x, g, b):
    xf = x.astype(jnp.float32)
    mu = jnp.mean(xf, axis=-1, keepdims=True)
    xc = xf - mu
    y = xc * lax.rsqrt(jnp.mean(xc * xc, axis=-1, keepdims=True) + EPS)
    return (y * g.astype(jnp.float32) + b.astype(jnp.float32)).astype(x.dtype)


def _project(h, w_in, q_norm, k_norm):
    z = h @ w_in
    q, k, v, a, gate = jnp.split(
        z, [ATTN_W, ATTN_W + KV_W, ATTN_W + 2 * KV_W, ATTN_W + 2 * KV_W + CONV_CH], axis=-1)
    lead = h.shape[:-1]
    q = _rmsnorm(q.reshape(*lead, N_KV_HEADS, GROUP, HEAD_DIM), q_norm)
    k = _rmsnorm(k.reshape(*lead, N_KV_HEADS, HEAD_DIM), k_norm)
    v = v.reshape(*lead, N_KV_HEADS, HEAD_DIM)
    u = a * jax.nn.sigmoid(gate)
    return q, k, v, u


def _sink_attn(q, k, v, q_pos, k_pos, sinks):
    s = jnp.einsum('...qhgd,...khd->...hgqk', q, k,
                   preferred_element_type=jnp.float32) * ATTN_SCALE
    kp = k_pos[..., None, :]
    qp = q_pos[..., :, None]
    mask = (kp <= qp) & (kp > qp - WINDOW) & (kp >= 0)
    s = jnp.where(mask[..., None, None, :, :], s, NEG)
    sink = sinks.astype(jnp.float32).reshape(N_KV_HEADS, GROUP, 1, 1)
    sink = jnp.broadcast_to(sink, s.shape[:-1] + (1,))
    p = jax.nn.softmax(jnp.concatenate([s, sink], axis=-1), axis=-1)[..., :-1]
    return jnp.einsum('...hgqk,...khd->...qhgd', p.astype(v.dtype), v)


def _conv_tail(u_ext, w_dw, b_dw, g_ln, b_ln):
    y = lax.conv_general_dilated(
        u_ext, w_dw[:, None, :].astype(u_ext.dtype), window_strides=(1,), padding='VALID',
        dimension_numbers=('NWC', 'WIO', 'NWC'), feature_group_count=CONV_CH)
    y = _layernorm(y + b_dw, g_ln, b_ln)
    return jax.nn.silu(y)


def _merge_and_mlp(x, attn_o, conv_o, w_out, g_mlp, w_up, w_down):
    x1 = x + jnp.concatenate([attn_o, conv_o], axis=-1) @ w_out
    h = _rmsnorm(x1, g_mlp)
    return x1 + jnp.square(jax.nn.relu(h @ w_up)) @ w_down


def setup_inputs(seed: int = 0) -> dict:
    key = jax.random.key(seed)
    ks = jax.random.split(key, 20)
    f = jnp.float32
    n = lambda k, shp, sc: (jax.random.normal(k, shp, f) * sc)
    return {
        "x_prompt": n(ks[0], (BATCH, SEQ, D_MODEL), 1.0),
        "x_sample": n(ks[1], (DEC_BATCH, DEC_SEQ, D_MODEL), 1.0),
        "cache_k": n(ks[2], (DEPTH, DEC_BATCH, CACHE_WIN, N_KV_HEADS, HEAD_DIM), 1.0),
        "cache_v": n(ks[3], (DEPTH, DEC_BATCH, CACHE_WIN, N_KV_HEADS, HEAD_DIM), 1.0),
        "state_conv": n(ks[4], (DEPTH, DEC_BATCH, CONV_BUF, CONV_CH), 0.5),
        "g_mix_norm": 1.0 + n(ks[5], (DEPTH, D_MODEL), 0.02),
        "w_in": n(ks[6], (DEPTH, D_MODEL, IN_W), D_MODEL ** -0.5),
        "q_norm": 1.0 + n(ks[7], (DEPTH, HEAD_DIM), 0.02),
        "k_norm": 1.0 + n(ks[8], (DEPTH, HEAD_DIM), 0.02),
        "sinks": n(ks[9], (DEPTH, N_HEADS), 0.5),
        "w_dw": n(ks[10], (DEPTH, CONV_W, CONV_CH), CONV_W ** -0.5),
        "b_dw": n(ks[11], (DEPTH, CONV_CH), 0.02),
        "g_conv_ln": 1.0 + n(ks[12], (DEPTH, CONV_CH), 0.02),
        "b_conv_ln": n(ks[13], (DEPTH, CONV_CH), 0.02),
        "w_out": n(ks[14], (DEPTH, MIX_W, D_MODEL), MIX_W ** -0.5),
        "g_mlp_norm": 1.0 + n(ks[15], (DEPTH, D_MODEL), 0.02),
        "w_up": n(ks[16], (DEPTH, D_MODEL, D_FF), D_MODEL ** -0.5),
        "w_down": n(ks[17], (DEPTH, D_FF, D_MODEL), D_FF ** -0.5),
    }


def reference(x_prompt, x_sample, cache_k, cache_v, state_conv,
              g_mix_norm, w_in, q_norm, k_norm, sinks, w_dw, b_dw, g_conv_ln, b_conv_ln,
              w_out, g_mlp_norm, w_up, w_down):
    B, S, _ = x_prompt.shape
    Bd, Sd, _ = x_sample.shape
    n_blk = S // BLOCK
    q_pos_p = jnp.arange(S, dtype=jnp.int32).reshape(n_blk, BLOCK)
    k_pos_p = q_pos_p[:, :1] - BLOCK + jnp.arange(2 * BLOCK, dtype=jnp.int32)
    q_pos_s = PAST_LEN + jnp.arange(Sd, dtype=jnp.int32)
    k_pos_s = PAST_LEN - CACHE_WIN + jnp.arange(CACHE_WIN + Sd, dtype=jnp.int32)

    xp, xs = x_prompt, x_sample
    nk_p, nv_p, nc_p, nk_s, nv_s, nc_s = [], [], [], [], [], []
    for l in range(DEPTH):
        h = _rmsnorm(xp, g_mix_norm[l])
        q, k, v, u = _project(h, w_in[l], q_norm[l], k_norm[l])
        qb = q.reshape(B, n_blk, BLOCK, N_KV_HEADS, GROUP, HEAD_DIM)
        kb = k.reshape(B, n_blk, BLOCK, N_KV_HEADS, HEAD_DIM)
        vb = v.reshape(B, n_blk, BLOCK, N_KV_HEADS, HEAD_DIM)
        pad = jnp.zeros_like(kb[:, :1])
        kk = jnp.concatenate([jnp.concatenate([pad, kb[:, :-1]], axis=1), kb], axis=2)
        vv = jnp.concatenate([jnp.concatenate([pad, vb[:, :-1]], axis=1), vb], axis=2)
        attn_p = _sink_attn(qb, kk, vv, q_pos_p, k_pos_p, sinks[l]).reshape(B, S, ATTN_W)
        u_ext = jnp.concatenate([jnp.zeros((B, CONV_W - 1, CONV_CH), u.dtype), u], axis=1)
        conv_p = _conv_tail(u_ext, w_dw[l], b_dw[l], g_conv_ln[l], b_conv_ln[l])
        xp = _merge_and_mlp(xp, attn_p, conv_p, w_out[l], g_mlp_norm[l], w_up[l], w_down[l])
        nk_p.append(k[:, S - CACHE_WIN:])
        nv_p.append(v[:, S - CACHE_WIN:])
        nc_p.append(u[:, S - CONV_BUF:])

        h = _rmsnorm(xs, g_mix_norm[l])
        q, k, v, u = _project(h, w_in[l], q_norm[l], k_norm[l])
        kk = jnp.concatenate([cache_k[l].astype(k.dtype), k], axis=1)
        vv = jnp.concatenate([cache_v[l].astype(v.dtype), v], axis=1)
        attn_s = _sink_attn(q, kk, vv, q_pos_s, k_pos_s, sinks[l]).reshape(Bd, Sd, ATTN_W)
        u_ext = jnp.concatenate([state_conv[l].astype(u.dtype), u], axis=1)
        conv_s = _conv_tail(u_ext[:, CONV_BUF - (CONV_W - 1):], w_dw[l], b_dw[l],
                            g_conv_ln[l], b_conv_ln[l])
        xs = _merge_and_mlp(xs, attn_s, conv_s, w_out[l], g_mlp_norm[l], w_up[l], w_down[l])
        nk_s.append(kk[:, Sd:])
        nv_s.append(vv[:, Sd:])
        nc_s.append(u_ext[:, Sd:])

    return (xp, xs, jnp.stack(nk_p), jnp.stack(nv_p), jnp.stack(nc_p),
            jnp.stack(nk_s), jnp.stack(nv_s), jnp.stack(nc_s))
```

```python
import functools

import jax
import jax.numpy as jnp
from jax import lax
from jax.experimental import pallas as pl
from jax.experimental.pallas import tpu as pltpu

F32 = jnp.float32
BF16 = jnp.bfloat16

D_MODEL = 2048
HEAD_DIM = 64
ATTN_W = D_MODEL // 2
N_HEADS = ATTN_W // HEAD_DIM
N_KV_HEADS = N_HEADS // 4
KV_W = N_KV_HEADS * HEAD_DIM
CONV_CH = D_MODEL - ATTN_W
IN_W = ATTN_W + 2 * KV_W + 2 * CONV_CH
WINDOW = 128
BLOCK = 128
CONV_W = 31
CONV_BUF = CONV_W - 1
D_FF = 4 * D_MODEL
EPS = 1e-6
ATTN_SCALE = HEAD_DIM ** -0.5
NEG = -1e30
PAST_LEN = 16384

LANES = 128
HEADS_PER_CHUNK = LANES // HEAD_DIM
Q_CHUNKS = ATTN_W // LANES
KV_CHUNKS = KV_W // LANES
CONV_HALO = 32
VMEM_LIMIT = 56 * 1024 * 1024


def _dot(a, b):
    return jnp.dot(a, b, preferred_element_type=F32)


def _dot_nt(a, b):
    return lax.dot_general(a, b, (((1,), (1,)), ((), ())), preferred_element_type=F32)


def _params(n_axes):
    return pltpu.CompilerParams(dimension_semantics=("arbitrary",) * n_axes,
                                vmem_limit_bytes=VMEM_LIMIT)


def _resident(shape):
    return pl.BlockSpec(shape, lambda *_: (0,) * len(shape), pipeline_mode=pl.Buffered(1))


def _rms(x, gain):
    ms = jnp.mean(x * x, axis=-1, keepdims=True)
    return x * lax.rsqrt(ms + EPS) * gain


def _head_rms_chunk(zc, is_lo):
    sq = zc * zc
    sq_lo = jnp.where(is_lo, sq, 0.0)
    sum_lo = jnp.sum(sq_lo, axis=-1, keepdims=True)
    sum_hi = jnp.sum(sq - sq_lo, axis=-1, keepdims=True)
    ms = jnp.where(is_lo, sum_lo, sum_hi) * (1.0 / HEAD_DIM)
    return zc * lax.rsqrt(ms + EPS)


def _proj_kernel(x_ref, g_ref, w_ref, qg_ref, kg_ref, q_ref, k_ref, v_ref, u_ref):
    tm = x_ref.shape[0]
    h = _rms(x_ref[...], g_ref[...]).astype(BF16)
    is_lo = lax.broadcasted_iota(jnp.int32, (tm, LANES), 1) < HEAD_DIM

    zq = _dot(h, w_ref[:, :ATTN_W])
    qg = qg_ref[...] * ATTN_SCALE
    for c in range(Q_CHUNKS):
        sl = slice(c * LANES, (c + 1) * LANES)
        q_ref[:, sl] = (_head_rms_chunk(zq[:, sl], is_lo) * qg).astype(BF16)

    zk = _dot(h, w_ref[:, ATTN_W:ATTN_W + KV_W])
    for c in range(KV_CHUNKS):
        sl = slice(c * LANES, (c + 1) * LANES)
        k_ref[:, sl] = _head_rms_chunk(zk[:, sl], is_lo) * kg_ref[...]

    v_ref[...] = _dot(h, w_ref[:, ATTN_W + KV_W:ATTN_W + 2 * KV_W])

    a0 = ATTN_W + 2 * KV_W
    a = _dot(h, w_ref[:, a0:a0 + CONV_CH])
    gate = _dot(h, w_ref[:, a0 + CONV_CH:])
    u_ref[...] = a * jax.nn.sigmoid(gate)


def _proj(x, g_mix, w_in, qg, kg, tm):
    t = x.shape[0]
    row = lambda w: pl.BlockSpec((tm, w), lambda i: (i, 0))
    return pl.pallas_call(
        _proj_kernel,
        grid=(t // tm,),
        in_specs=[row(D_MODEL), _resident((1, D_MODEL)), _resident((D_MODEL, IN_W)),
                  _resident((1, LANES)), _resident((1, LANES))],
        out_specs=[row(ATTN_W), row(KV_W), row(KV_W), row(CONV_CH)],
        out_shape=[jax.ShapeDtypeStruct((t, ATTN_W), BF16),
                   jax.ShapeDtypeStruct((t, KV_W), F32),
                   jax.ShapeDtypeStruct((t, KV_W), F32),
                   jax.ShapeDtypeStruct((t, CONV_CH), F32)],
        compiler_params=_params(1),
        name="proj",
    )(x, g_mix, w_in, qg, kg)


def _split_heads(chunk, is_lo):
    swapped = pltpu.roll(chunk, HEAD_DIM, axis=1)
    head_a = (jnp.where(is_lo, chunk, 0.0), jnp.where(is_lo, 0.0, swapped))
    head_b = (jnp.where(is_lo, swapped, 0.0), jnp.where(is_lo, 0.0, chunk))
    return head_a, head_b


def _softmax_parts(s_list, sink):
    m = sink
    for s in s_list:
        m = jnp.maximum(m, jnp.max(s, axis=-1, keepdims=True))
    p_list = [jnp.exp(s - m) for s in s_list]
    den = jnp.exp(sink - m)
    for p in p_list:
        den = den + jnp.sum(p, axis=-1, keepdims=True)
    return p_list, 1.0 / den


def _attn_prompt_kernel(sink_ref, q_ref, k_ref, v_ref, o_ref, kk_ref, vv_ref):
    n_blk = q_ref.shape[0] // BLOCK
    is_lo = lax.broadcasted_iota(jnp.int32, (BLOCK, LANES), 1) < HEAD_DIM
    tri = (lax.broadcasted_iota(jnp.int32, (BLOCK, BLOCK), 1)
           <= lax.broadcasted_iota(jnp.int32, (BLOCK, BLOCK), 0))

    def build(n, carry):
        rows = pl.ds(pl.multiple_of(n * BLOCK, BLOCK), BLOCK)
        for c in range(KV_CHUNKS):
            sl = slice(c * LANES, (c + 1) * LANES)
            for src, dst in ((k_ref, kk_ref), (v_ref, vv_ref)):
                heads = _split_heads(src[rows, sl], is_lo)
                for half, (lo, hi) in enumerate(heads):
                    g = HEADS_PER_CHUNK * c + half
                    dst[g, n, :BLOCK, :] = lo.astype(BF16)
                    dst[g, n, BLOCK:, :] = hi.astype(BF16)
        return carry

    lax.fori_loop(0, n_blk, build, 0)

    def block(n, has_prev):
        rows = pl.ds(pl.multiple_of(n * BLOCK, BLOCK), BLOCK)
        for g in range(N_KV_HEADS):
            kk_own, vv_own = kk_ref[g, n], vv_ref[g, n]
            if has_prev:
                kk_prev, vv_prev = kk_ref[g, n - 1], vv_ref[g, n - 1]
            for jj in range(HEADS_PER_CHUNK):
                j = HEADS_PER_CHUNK * g + jj
                sl = slice(j * LANES, (j + 1) * LANES)
                qj = q_ref[rows, sl]
                s_own = _dot_nt(qj, kk_own)
                s_prev = _dot_nt(qj, kk_prev) if has_prev else None
                p_own, p_prev, inv = [], [], []
                for e in range(HEADS_PER_CHUNK):
                    es = slice(e * BLOCK, (e + 1) * BLOCK)
                    s = jnp.where(tri, s_own[:, es], s_prev[:, es] if has_prev else NEG)
                    (p,), r = _softmax_parts([s], sink_ref[HEADS_PER_CHUNK * j + e])
                    po = jnp.where(tri, p, 0.0)
                    p_own.append(po.astype(BF16))
                    p_prev.append((p - po).astype(BF16))
                    inv.append(r)
                o = _dot(jnp.concatenate(p_own, axis=1), vv_own)
                if has_prev:
                    o = o + _dot(jnp.concatenate(p_prev, axis=1), vv_prev)
                o_ref[rows, sl] = (o * jnp.where(is_lo, inv[0], inv[1])).astype(BF16)

    block(0, False)

    def body(n, carry):
        block(n, True)
        return carry

    lax.fori_loop(1, n_blk, body, 0)


def _attn_prompt(sinks, q, k, v, batch, seq):
    n_blk = seq // BLOCK
    row = lambda w: pl.BlockSpec((seq, w), lambda b: (b, 0))
    return pl.pallas_call(
        _attn_prompt_kernel,
        grid=(batch,),
        in_specs=[pl.BlockSpec(memory_space=pltpu.SMEM), row(ATTN_W), row(KV_W), row(KV_W)],
        out_specs=row(ATTN_W),
        out_shape=jax.ShapeDtypeStruct((batch * seq, ATTN_W), BF16),
        scratch_shapes=[pltpu.VMEM((N_KV_HEADS, n_blk, 2 * BLOCK, LANES), BF16),
                        pltpu.VMEM((N_KV_HEADS, n_blk, 2 * BLOCK, LANES), BF16)],
        compiler_params=_params(1),
        name="attn_prompt",
    )(sinks, q, k, v)


def _attn_sample_kernel(sink_ref, q_ref, k_ref, v_ref, ck_ref, cv_ref,
                        o_ref, nk_ref, nv_ref, kk_ref, vv_ref, *, dec_seq):
    bb = ck_ref.shape[0]
    rows = bb * dec_seq
    win = ck_ref.shape[1]
    is_lo = lax.broadcasted_iota(jnp.int32, (win, LANES), 1) < HEAD_DIM
    is_lo_r = lax.broadcasted_iota(jnp.int32, (rows, LANES), 1) < HEAD_DIM

    for b in range(bb):
        new = slice(b * dec_seq, (b + 1) * dec_seq)
        nk_ref[b, :win - dec_seq, :] = ck_ref[b, dec_seq:, :]
        nk_ref[b, win - dec_seq:, :] = k_ref[new, :]
        nv_ref[b, :win - dec_seq, :] = cv_ref[b, dec_seq:, :]
        nv_ref[b, win - dec_seq:, :] = v_ref[new, :]

    for b in range(bb):
        for c in range(KV_CHUNKS):
            sl = slice(c * LANES, (c + 1) * LANES)
            for src, dst in ((ck_ref, kk_ref), (cv_ref, vv_ref)):
                for half, (lo, hi) in enumerate(_split_heads(src[b, :, sl], is_lo)):
                    g = HEADS_PER_CHUNK * c + half
                    dst[b, g, :win, :] = lo.astype(BF16)
                    dst[b, g, win:, :] = hi.astype(BF16)

    r_id = lax.broadcasted_iota(jnp.int32, (rows, win), 0)
    c_id = lax.broadcasted_iota(jnp.int32, (rows, win), 1)
    win_mask = c_id > r_id % dec_seq
    rn = lax.broadcasted_iota(jnp.int32, (rows, rows), 0)
    cn = lax.broadcasted_iota(jnp.int32, (rows, rows), 1)
    new_mask = (rn // dec_seq == cn // dec_seq) & (cn % dec_seq <= rn % dec_seq)
    seq_of_row = lax.broadcasted_iota(jnp.int32, (rows, LANES), 0) // dec_seq

    for g in range(N_KV_HEADS):
        c, half = divmod(g, HEADS_PER_CHUNK)
        sl_kv = slice(c * LANES, (c + 1) * LANES)
        kn = _split_heads(k_ref[:, sl_kv], is_lo_r)[half]
        vn = _split_heads(v_ref[:, sl_kv], is_lo_r)[half]
        kk_new = jnp.concatenate([kn[0], kn[1]], axis=0).astype(BF16)
        vv_new = jnp.concatenate([vn[0], vn[1]], axis=0).astype(BF16)
        for jj in range(HEADS_PER_CHUNK):
            j = HEADS_PER_CHUNK * g + jj
            sl = slice(j * LANES, (j + 1) * LANES)
            qj = q_ref[:, sl]
            s_new = _dot_nt(qj, kk_new)
            s_win = jnp.zeros((rows, 2 * win), F32)
            for b in range(bb):
                sb = _dot_nt(qj, kk_ref[b, g])
                s_win = jnp.where(jnp.concatenate([seq_of_row, seq_of_row], axis=1) == b, sb, s_win)
            p_win, p_new, inv = [], [], []
            for e in range(HEADS_PER_CHUNK):
                sw = jnp.where(win_mask, s_win[:, e * win:(e + 1) * win], NEG)
                sn = jnp.where(new_mask, s_new[:, e * rows:(e + 1) * rows], NEG)
                (pw, pn), r = _softmax_parts([sw, sn], sink_ref[HEADS_PER_CHUNK * j + e])
                p_win.append(pw.astype(BF16))
                p_new.append(pn.astype(BF16))
                inv.append(r)
            p_win = jnp.concatenate(p_win, axis=1)
            o = _dot(jnp.concatenate(p_new, axis=1), vv_new)
            for b in range(bb):
                o = o + jnp.where(seq_of_row == b, _dot(p_win, vv_ref[b, g]), 0.0)
            o_ref[:, sl] = (o * jnp.where(is_lo_r, inv[0], inv[1])).astype(BF16)


def _attn_sample(sinks, q, k, v, cache_k, cache_v, dec_seq, bb):
    dec_batch, win, _ = cache_k.shape
    rows = bb * dec_seq
    row = lambda w: pl.BlockSpec((rows, w), lambda i: (i, 0))
    cache = pl.BlockSpec((bb, win, KV_W), lambda i: (i, 0, 0))
    return pl.pallas_call(
        functools.partial(_attn_sample_kernel, dec_seq=dec_seq),
        grid=(dec_batch // bb,),
        in_specs=[pl.BlockSpec(memory_space=pltpu.SMEM), row(ATTN_W), row(KV_W), row(KV_W), cache, cache],
        out_specs=[row(ATTN_W), cache, cache],
        out_shape=[jax.ShapeDtypeStruct((dec_batch * dec_seq, ATTN_W), BF16),
                   jax.ShapeDtypeStruct(cache_k.shape, F32),
                   jax.ShapeDtypeStruct(cache_v.shape, F32)],
        scratch_shapes=[pltpu.VMEM((bb, N_KV_HEADS, 2 * win, LANES), BF16),
                        pltpu.VMEM((bb, N_KV_HEADS, 2 * win, LANES), BF16)],
        compiler_params=_params(1),
        name="attn_sample",
    )(sinks, q, k, v, cache_k, cache_v)


def _ln_swish(y, g, b):
    mu = jnp.mean(y, axis=-1, keepdims=True)
    yc = y - mu
    yn = yc * lax.rsqrt(jnp.mean(yc * yc, axis=-1, keepdims=True) + EPS) * g + b
    return yn * jax.nn.sigmoid(yn)


def _conv_prompt_kernel(u_ref, w_ref, b_ref, g_ref, bl_ref, o_ref, ext_ref, *, sub_rows):
    tr = u_ref.shape[0]
    i = pl.program_id(1)

    @pl.when(i == 0)
    def _():
        ext_ref[:CONV_HALO, :] = jnp.zeros((CONV_HALO, CONV_CH), F32)

    @pl.when(i > 0)
    def _():
        ext_ref[:CONV_HALO, :] = ext_ref[tr:tr + CONV_HALO, :]

    ext_ref[CONV_HALO:, :] = u_ref[...]

    for r0 in range(0, tr, sub_rows):
        acc = jnp.zeros((sub_rows, CONV_CH), F32)
        for j in range(CONV_W):
            start = CONV_HALO - CONV_BUF + j + r0
            acc = acc + w_ref[j:j + 1, :] * ext_ref[start:start + sub_rows, :]
        y = _ln_swish(acc + b_ref[...], g_ref[...], bl_ref[...])
        o_ref[r0:r0 + sub_rows, :] = y.astype(BF16)


def _conv_prompt(u, w_dw, b_dw, g_ln, b_ln, batch, seq, tr):
    n_t = seq // tr
    tile = pl.BlockSpec((tr, CONV_CH), lambda b, i: (b * n_t + i, 0))
    vec = _resident((1, CONV_CH))
    return pl.pallas_call(
        functools.partial(_conv_prompt_kernel, sub_rows=16),
        grid=(batch, n_t),
        in_specs=[tile, _resident((CONV_W, CONV_CH)), vec, vec, vec],
        out_specs=tile,
        out_shape=jax.ShapeDtypeStruct((batch * seq, CONV_CH), BF16),
        scratch_shapes=[pltpu.VMEM((CONV_HALO + tr, CONV_CH), F32)],
        compiler_params=_params(2),
        name="conv_prompt",
    )(u, w_dw, b_dw, g_ln, b_ln)


def _conv_sample_kernel(u_ref, st_ref, w_ref, b_ref, g_ref, bl_ref, o_ref, ns_ref, ext_ref, y_ref,
                        *, dec_seq):
    bb = st_ref.shape[0]
    for b in range(bb):
        new = slice(b * dec_seq, (b + 1) * dec_seq)
        ext_ref[:CONV_BUF, :] = st_ref[b]
        ext_ref[CONV_BUF:CONV_BUF + dec_seq, :] = u_ref[new, :]
        ns_ref[b] = ext_ref[dec_seq:dec_seq + CONV_BUF, :]
        acc = jnp.zeros((dec_seq, CONV_CH), F32)
        for j in range(CONV_W):
            acc = acc + w_ref[j:j + 1, :] * ext_ref[j:j + dec_seq, :]
        y_ref[new, :] = acc
    o_ref[...] = _ln_swish(y_ref[...] + b_ref[...], g_ref[...], bl_ref[...]).astype(BF16)


def _conv_sample(u, state, w_dw, b_dw, g_ln, b_ln, dec_seq, bb):
    dec_batch = state.shape[0]
    rows = bb * dec_seq
    tile = pl.BlockSpec((rows, CONV_CH), lambda i: (i, 0))
    st = pl.BlockSpec((bb, CONV_BUF, CONV_CH), lambda i: (i, 0, 0))
    vec = _resident((1, CONV_CH))
    return pl.pallas_call(
        functools.partial(_conv_sample_kernel, dec_seq=dec_seq),
        grid=(dec_batch // bb,),
        in_specs=[tile, st, _resident((CONV_W, CONV_CH)), vec, vec, vec],
        out_specs=[tile, st],
        out_shape=[jax.ShapeDtypeStruct((dec_batch * dec_seq, CONV_CH), BF16),
                   jax.ShapeDtypeStruct(state.shape, F32)],
        scratch_shapes=[pltpu.VMEM((CONV_BUF + 2 * dec_seq + 8, CONV_CH), F32),
                        pltpu.VMEM((rows, CONV_CH), F32)],
        compiler_params=_params(1),
        name="conv_sample",
    )(u, state, w_dw, b_dw, g_ln, b_ln)


def _mlp_kernel(x_ref, a_ref, c_ref, wo_ref, g_ref, wu_ref, wd_ref, o_ref, h_ref):
    @pl.when(pl.program_id(1) == 0)
    def _():
        mix = jnp.concatenate([a_ref[...], c_ref[...]], axis=1)
        x1 = x_ref[...] + _dot(mix, wo_ref[...])
        o_ref[...] = x1
        h_ref[...] = _rms(x1, g_ref[...]).astype(BF16)

    up = _dot(h_ref[...], wu_ref[...])
    act = jnp.square(jnp.maximum(up, 0.0)).astype(BF16)
    o_ref[...] += _dot(act, wd_ref[...])


def _mlp(x, attn_o, conv_o, w_out, g_mlp, w_up, w_down, tm, tf):
    t = x.shape[0]
    row = lambda w: pl.BlockSpec((tm, w), lambda i, j: (i, 0))
    return pl.pallas_call(
        _mlp_kernel,
        grid=(t // tm, D_FF // tf),
        in_specs=[row(D_MODEL), row(ATTN_W), row(CONV_CH),
                  _resident((D_MODEL, D_MODEL)), _resident((1, D_MODEL)),
                  pl.BlockSpec((D_MODEL, tf), lambda i, j: (0, j)),
                  pl.BlockSpec((tf, D_MODEL), lambda i, j: (j, 0))],
        out_specs=row(D_MODEL),
        out_shape=jax.ShapeDtypeStruct((t, D_MODEL), F32),
        scratch_shapes=[pltpu.VMEM((tm, D_MODEL), BF16)],
        compiler_params=_params(2),
        name="merge_mlp",
    )(x, attn_o, conv_o, w_out, g_mlp, w_up, w_down)


def _tile_heads(gain):
    return jnp.tile(gain.reshape(1, HEAD_DIM), (1, HEADS_PER_CHUNK))


def kernel(x_prompt, x_sample, cache_k, cache_v, state_conv, g_mix_norm, w_in, q_norm, k_norm, sinks,
           w_dw, b_dw, g_conv_ln, b_conv_ln, w_out, g_mlp_norm, w_up, w_down):
    batch, seq, _ = x_prompt.shape
    dec_batch, dec_seq, _ = x_sample.shape
    depth = w_in.shape[0]
    win = cache_k.shape[2]
    assert win == WINDOW == BLOCK and seq % BLOCK == 0 and PAST_LEN >= win

    xp = x_prompt.reshape(batch * seq, D_MODEL)
    xs = x_sample.reshape(dec_batch * dec_seq, D_MODEL)
    tm_p = min(512, batch * seq)
    tm_s = min(512, dec_batch * dec_seq)
    bb = min(8, dec_batch)
    outs = [[] for _ in range(6)]
    for l in range(depth):
        w_in_l, w_out_l = w_in[l].astype(BF16), w_out[l].astype(BF16)
        w_up_l, w_down_l = w_up[l].astype(BF16), w_down[l].astype(BF16)
        g_mix, g_mlp = g_mix_norm[l].reshape(1, D_MODEL), g_mlp_norm[l].reshape(1, D_MODEL)
        qg, kg = _tile_heads(q_norm[l]), _tile_heads(k_norm[l])
        conv_vecs = (w_dw[l], b_dw[l].reshape(1, CONV_CH), g_conv_ln[l].reshape(1, CONV_CH),
                     b_conv_ln[l].reshape(1, CONV_CH))

        q, k, v, u = _proj(xp, g_mix, w_in_l, qg, kg, tm_p)
        attn_o = _attn_prompt(sinks[l], q, k, v, batch, seq)
        conv_o = _conv_prompt(u, *conv_vecs, batch, seq, min(256, seq))
        xp = _mlp(xp, attn_o, conv_o, w_out_l, g_mlp, w_up_l, w_down_l, tm_p, 512)
        outs[0].append(k.reshape(batch, seq, N_KV_HEADS, HEAD_DIM)[:, seq - win:])
        outs[1].append(v.reshape(batch, seq, N_KV_HEADS, HEAD_DIM)[:, seq - win:])
        outs[2].append(u.reshape(batch, seq, CONV_CH)[:, seq - CONV_BUF:])

        q, k, v, u = _proj(xs, g_mix, w_in_l, qg, kg, tm_s)
        attn_o, nk, nv = _attn_sample(sinks[l], q, k, v,
                                      cache_k[l].reshape(dec_batch, win, KV_W),
                                      cache_v[l].reshape(dec_batch, win, KV_W), dec_seq, bb)
        conv_o, ns = _conv_sample(u, state_conv[l], *conv_vecs, dec_seq, bb)
        xs = _mlp(xs, attn_o, conv_o, w_out_l, g_mlp, w_up_l, w_down_l, tm_s, 512)
        outs[3].append(nk.reshape(dec_batch, win, N_KV_HEADS, HEAD_DIM))
        outs[4].append(nv.reshape(dec_batch, win, N_KV_HEADS, HEAD_DIM))
        outs[5].append(ns)

    return (xp.reshape(batch, seq, D_MODEL), xs.reshape(dec_batch, dec_seq, D_MODEL),
            *(jnp.stack(o) for o in outs))
```

```python
import functools

import jax
import jax.numpy as jnp
from jax import lax
from jax.experimental import pallas as pl
from jax.experimental.pallas import tpu as pltpu

F32 = jnp.float32
BF16 = jnp.bfloat16

D_MODEL = 2048
HEAD_DIM = 64
ATTN_W = D_MODEL // 2
N_HEADS = ATTN_W // HEAD_DIM
N_KV_HEADS = N_HEADS // 4
KV_W = N_KV_HEADS * HEAD_DIM
CONV_CH = D_MODEL - ATTN_W
IN_W = ATTN_W + 2 * KV_W + 2 * CONV_CH
WINDOW = 128
BLOCK = 128
CONV_W = 31
CONV_BUF = CONV_W - 1
D_FF = 4 * D_MODEL
EPS = 1e-6
ATTN_SCALE = HEAD_DIM ** -0.5
NEG = -1e30
PAST_LEN = 16384

LANES = 128
SUBLANES = 8
HEADS_PER_CHUNK = LANES // HEAD_DIM
Q_CHUNKS = ATTN_W // LANES
KV_CHUNKS = KV_W // LANES
CONV_HALO = 32
VMEM_LIMIT = 56 * 1024 * 1024


def _dot(a, b):
    return jnp.dot(a, b, preferred_element_type=F32)


def _dot_nt(a, b):
    return lax.dot_general(a, b, (((1,), (1,)), ((), ())), preferred_element_type=F32)


def _params(n_axes):
    return pltpu.CompilerParams(dimension_semantics=("arbitrary",) * n_axes,
                                vmem_limit_bytes=VMEM_LIMIT)


def _resident(shape):
    return pl.BlockSpec(shape, lambda *_: (0,) * len(shape), pipeline_mode=pl.Buffered(1))


def _rms(x, gain):
    ms = jnp.mean(x * x, axis=-1, keepdims=True)
    return x * lax.rsqrt(ms + EPS) * gain


def _head_rms_chunk(zc, is_lo):
    sq = zc * zc
    sq_lo = jnp.where(is_lo, sq, 0.0)
    sum_lo = jnp.sum(sq_lo, axis=-1, keepdims=True)
    sum_hi = jnp.sum(sq - sq_lo, axis=-1, keepdims=True)
    ms = jnp.where(is_lo, sum_lo, sum_hi) * (1.0 / HEAD_DIM)
    return zc * lax.rsqrt(ms + EPS)


def _proj_kernel(x_ref, g_ref, w_ref, qg_ref, kg_ref, q_ref, k_ref, v_ref, u_ref):
    tm = x_ref.shape[0]
    h = _rms(x_ref[...], g_ref[...]).astype(BF16)
    is_lo = lax.broadcasted_iota(jnp.int32, (tm, LANES), 1) < HEAD_DIM

    zq = _dot(h, w_ref[:, :ATTN_W])
    qg = qg_ref[...] * ATTN_SCALE
    for c in range(Q_CHUNKS):
        sl = slice(c * LANES, (c + 1) * LANES)
        q_ref[:, sl] = (_head_rms_chunk(zq[:, sl], is_lo) * qg).astype(BF16)

    zk = _dot(h, w_ref[:, ATTN_W:ATTN_W + KV_W])
    for c in range(KV_CHUNKS):
        sl = slice(c * LANES, (c + 1) * LANES)
        k_ref[:, sl] = _head_rms_chunk(zk[:, sl], is_lo) * kg_ref[...]

    v_ref[...] = _dot(h, w_ref[:, ATTN_W + KV_W:ATTN_W + 2 * KV_W])

    a0 = ATTN_W + 2 * KV_W
    a = _dot(h, w_ref[:, a0:a0 + CONV_CH])
    gate = _dot(h, w_ref[:, a0 + CONV_CH:])
    u_ref[...] = a * jax.nn.sigmoid(gate)


def _proj(x, g_mix, w_in, qg, kg, tm):
    t = x.shape[0]
    row = lambda w: pl.BlockSpec((tm, w), lambda i: (i, 0))
    return pl.pallas_call(
        _proj_kernel,
        grid=(t // tm,),
        in_specs=[row(D_MODEL), _resident((1, D_MODEL)), _resident((D_MODEL, IN_W)),
                  _resident((1, LANES)), _resident((1, LANES))],
        out_specs=[row(ATTN_W), row(KV_W), row(KV_W), row(CONV_CH)],
        out_shape=[jax.ShapeDtypeStruct((t, ATTN_W), BF16),
                   jax.ShapeDtypeStruct((t, KV_W), F32),
                   jax.ShapeDtypeStruct((t, KV_W), F32),
                   jax.ShapeDtypeStruct((t, CONV_CH), F32)],
        compiler_params=_params(1),
        name="proj",
    )(x, g_mix, w_in, qg, kg)


def _split_heads(chunk, is_lo):
    swapped = pltpu.roll(chunk, HEAD_DIM, axis=1)
    head_a = (jnp.where(is_lo, chunk, 0.0), jnp.where(is_lo, 0.0, swapped))
    head_b = (jnp.where(is_lo, swapped, 0.0), jnp.where(is_lo, 0.0, chunk))
    return head_a, head_b


def _softmax_parts(s_list, sink):
    m = sink
    for s in s_list:
        m = jnp.maximum(m, jnp.max(s, axis=-1, keepdims=True))
    p_list = [jnp.exp(s - m) for s in s_list]
    den = jnp.exp(sink - m)
    for p in p_list:
        den = den + jnp.sum(p, axis=-1, keepdims=True)
    return p_list, 1.0 / den


def _attn_prompt_kernel(sink_ref, q_ref, k_ref, v_ref, o_ref, kk_ref, vv_ref):
    n_blk = q_ref.shape[0] // BLOCK
    is_lo = lax.broadcasted_iota(jnp.int32, (BLOCK, LANES), 1) < HEAD_DIM
    tri = (lax.broadcasted_iota(jnp.int32, (BLOCK, BLOCK), 1)
           <= lax.broadcasted_iota(jnp.int32, (BLOCK, BLOCK), 0))

    def build(n, carry):
        rows = pl.ds(pl.multiple_of(n * BLOCK, BLOCK), BLOCK)
        for c in range(KV_CHUNKS):
            sl = slice(c * LANES, (c + 1) * LANES)
            for src, dst in ((k_ref, kk_ref), (v_ref, vv_ref)):
                heads = _split_heads(src[rows, sl], is_lo)
                for half, (lo, hi) in enumerate(heads):
                    g = HEADS_PER_CHUNK * c + half
                    dst[g, n, :BLOCK, :] = lo.astype(BF16)
                    dst[g, n, BLOCK:, :] = hi.astype(BF16)
        return carry

    lax.fori_loop(0, n_blk, build, 0)

    def block(n, has_prev):
        rows = pl.ds(pl.multiple_of(n * BLOCK, BLOCK), BLOCK)
        for g in range(N_KV_HEADS):
            kk_own, vv_own = kk_ref[g, n], vv_ref[g, n]
            if has_prev:
                kk_prev, vv_prev = kk_ref[g, n - 1], vv_ref[g, n - 1]
            for jj in range(HEADS_PER_CHUNK):
                j = HEADS_PER_CHUNK * g + jj
                sl = slice(j * LANES, (j + 1) * LANES)
                qj = q_ref[rows, sl]
                s_own = _dot_nt(qj, kk_own)
                s_prev = _dot_nt(qj, kk_prev) if has_prev else None
                p_own, p_prev, inv = [], [], []
                for e in range(HEADS_PER_CHUNK):
                    es = slice(e * BLOCK, (e + 1) * BLOCK)
                    s = jnp.where(tri, s_own[:, es], s_prev[:, es] if has_prev else NEG)
                    (p,), r = _softmax_parts([s], sink_ref[HEADS_PER_CHUNK * j + e])
                    po = jnp.where(tri, p, 0.0)
                    p_own.append(po.astype(BF16))
                    p_prev.append((p - po).astype(BF16))
                    inv.append(r)
                o = _dot(jnp.concatenate(p_own, axis=1), vv_own)
                if has_prev:
                    o = o + _dot(jnp.concatenate(p_prev, axis=1), vv_prev)
                o_ref[rows, sl] = (o * jnp.where(is_lo, inv[0], inv[1])).astype(BF16)

    block(0, False)

    def body(n, carry):
        block(n, True)
        return carry

    lax.fori_loop(1, n_blk, body, 0)


def _attn_prompt(sinks, q, k, v, batch, seq):
    n_blk = seq // BLOCK
    row = lambda w: pl.BlockSpec((seq, w), lambda b: (b, 0))
    return pl.pallas_call(
        _attn_prompt_kernel,
        grid=(batch,),
        in_specs=[pl.BlockSpec(memory_space=pltpu.SMEM), row(ATTN_W), row(KV_W), row(KV_W)],
        out_specs=row(ATTN_W),
        out_shape=jax.ShapeDtypeStruct((batch * seq, ATTN_W), BF16),
        scratch_shapes=[pltpu.VMEM((N_KV_HEADS, n_blk, 2 * BLOCK, LANES), BF16),
                        pltpu.VMEM((N_KV_HEADS, n_blk, 2 * BLOCK, LANES), BF16)],
        compiler_params=_params(1),
        name="attn_prompt",
    )(sinks, q, k, v)


def _attn_sample_kernel(sink_ref, q_ref, k_ref, v_ref, ck_ref, cv_ref,
                        o_ref, nk_ref, nv_ref, kk_ref, vv_ref, *, dec_seq):
    bb = ck_ref.shape[0]
    rows = bb * dec_seq
    win = ck_ref.shape[1]
    is_lo = lax.broadcasted_iota(jnp.int32, (win, LANES), 1) < HEAD_DIM
    is_lo_r = lax.broadcasted_iota(jnp.int32, (rows, LANES), 1) < HEAD_DIM

    for b in range(bb):
        new = slice(b * dec_seq, (b + 1) * dec_seq)
        nk_ref[b, :win - dec_seq, :] = ck_ref[b, dec_seq:, :]
        nk_ref[b, win - dec_seq:, :] = k_ref[new, :]
        nv_ref[b, :win - dec_seq, :] = cv_ref[b, dec_seq:, :]
        nv_ref[b, win - dec_seq:, :] = v_ref[new, :]

    for b in range(bb):
        for c in range(KV_CHUNKS):
            sl = slice(c * LANES, (c + 1) * LANES)
            for src, dst in ((ck_ref, kk_ref), (cv_ref, vv_ref)):
                for half, (lo, hi) in enumerate(_split_heads(src[b, :, sl], is_lo)):
                    g = HEADS_PER_CHUNK * c + half
                    dst[b, g, :win, :] = lo.astype(BF16)
                    dst[b, g, win:, :] = hi.astype(BF16)

    r_id = lax.broadcasted_iota(jnp.int32, (rows, win), 0)
    c_id = lax.broadcasted_iota(jnp.int32, (rows, win), 1)
    win_mask = c_id > r_id % dec_seq
    rn = lax.broadcasted_iota(jnp.int32, (rows, rows), 0)
    cn = lax.broadcasted_iota(jnp.int32, (rows, rows), 1)
    new_mask = (rn // dec_seq == cn // dec_seq) & (cn % dec_seq <= rn % dec_seq)
    seq_of_row = lax.broadcasted_iota(jnp.int32, (rows, LANES), 0) // dec_seq

    for g in range(N_KV_HEADS):
        c, half = divmod(g, HEADS_PER_CHUNK)
        sl_kv = slice(c * LANES, (c + 1) * LANES)
        kn = _split_heads(k_ref[:, sl_kv], is_lo_r)[half]
        vn = _split_heads(v_ref[:, sl_kv], is_lo_r)[half]
        kk_new = jnp.concatenate([kn[0], kn[1]], axis=0).astype(BF16)
        vv_new = jnp.concatenate([vn[0], vn[1]], axis=0).astype(BF16)
        for jj in range(HEADS_PER_CHUNK):
            j = HEADS_PER_CHUNK * g + jj
            sl = slice(j * LANES, (j + 1) * LANES)
            qj = q_ref[:, sl]
            s_new = _dot_nt(qj, kk_new)
            s_win = jnp.zeros((rows, 2 * win), F32)
            for b in range(bb):
                sb = _dot_nt(qj, kk_ref[b, g])
                s_win = jnp.where(jnp.concatenate([seq_of_row, seq_of_row], axis=1) == b, sb, s_win)
            p_win, p_new, inv = [], [], []
            for e in range(HEADS_PER_CHUNK):
                sw = jnp.where(win_mask, s_win[:, e * win:(e + 1) * win], NEG)
                sn = jnp.where(new_mask, s_new[:, e * rows:(e + 1) * rows], NEG)
                (pw, pn), r = _softmax_parts([sw, sn], sink_ref[HEADS_PER_CHUNK * j + e])
                p_win.append(pw.astype(BF16))
                p_new.append(pn.astype(BF16))
                inv.append(r)
            p_win = jnp.concatenate(p_win, axis=1)
            o = _dot(jnp.concatenate(p_new, axis=1), vv_new)
            for b in range(bb):
                o = o + jnp.where(seq_of_row == b, _dot(p_win, vv_ref[b, g]), 0.0)
            o_ref[:, sl] = (o * jnp.where(is_lo_r, inv[0], inv[1])).astype(BF16)


def _attn_sample(sinks, q, k, v, cache_k, cache_v, dec_seq, bb):
    dec_batch, win, _ = cache_k.shape
    rows = bb * dec_seq
    row = lambda w: pl.BlockSpec((rows, w), lambda i: (i, 0))
    cache = pl.BlockSpec((bb, win, KV_W), lambda i: (i, 0, 0))
    return pl.pallas_call(
        functools.partial(_attn_sample_kernel, dec_seq=dec_seq),
        grid=(dec_batch // bb,),
        in_specs=[pl.BlockSpec(memory_space=pltpu.SMEM), row(ATTN_W), row(KV_W), row(KV_W), cache, cache],
        out_specs=[row(ATTN_W), cache, cache],
        out_shape=[jax.ShapeDtypeStruct((dec_batch * dec_seq, ATTN_W), BF16),
                   jax.ShapeDtypeStruct(cache_k.shape, F32),
                   jax.ShapeDtypeStruct(cache_v.shape, F32)],
        scratch_shapes=[pltpu.VMEM((bb, N_KV_HEADS, 2 * win, LANES), BF16),
                        pltpu.VMEM((bb, N_KV_HEADS, 2 * win, LANES), BF16)],
        compiler_params=_params(1),
        name="attn_sample",
    )(sinks, q, k, v, cache_k, cache_v)


def _ln_swish(y, g, b):
    mu = jnp.mean(y, axis=-1, keepdims=True)
    yc = y - mu
    yn = yc * lax.rsqrt(jnp.mean(yc * yc, axis=-1, keepdims=True) + EPS) * g + b
    return yn * jax.nn.sigmoid(yn)


def _conv_prompt_kernel(u_ref, w_ref, b_ref, g_ref, bl_ref, o_ref, sh_ref, wb_ref, y_ref,
                        *, sub_rows, sub_cols):
    tr = u_ref.shape[0]
    n_col = CONV_CH // sub_cols
    i = pl.program_id(1)

    @pl.when(i == 0)
    def _():
        sh_ref[:, :, :CONV_HALO, :] = jnp.zeros((SUBLANES, n_col, CONV_HALO, sub_cols), F32)

    @pl.when(i > 0)
    def _():
        sh_ref[:, :, :CONV_HALO, :] = sh_ref[:, :, tr:tr + CONV_HALO, :]

    @pl.when((pl.program_id(0) == 0) & (i == 0))
    def _():
        for tap in range(CONV_W):
            for c in range(n_col):
                wb_ref[tap, c] = jnp.broadcast_to(w_ref[tap:tap + 1, c * sub_cols:(c + 1) * sub_cols],
                                                  (SUBLANES, sub_cols))

    for c in range(n_col):
        sh_ref[0, c, CONV_HALO:, :] = u_ref[:, c * sub_cols:(c + 1) * sub_cols]
        x = sh_ref[0, c, CONV_HALO - SUBLANES:, :]
        for r in range(1, SUBLANES):
            sh_ref[r, c, CONV_HALO:, :] = pltpu.roll(x, r, axis=0)[SUBLANES:]

    groups = sub_rows // SUBLANES

    def piece(idx, carry):
        rb, c = idx // n_col, idx % n_col
        r0 = pl.multiple_of(rb * sub_rows, sub_rows)
        acc = [jnp.zeros((SUBLANES, sub_cols), F32)] * groups
        for r in range(SUBLANES):
            n_a = (CONV_W - 1 - r) // SUBLANES + 1
            w_taps = [wb_ref[CONV_W - 1 - (SUBLANES * a + r), c] for a in range(n_a)]
            for m in range(1 - n_a, groups):
                start = pl.multiple_of(r0 + (CONV_HALO + SUBLANES * m), SUBLANES)
                x = sh_ref[r, c, pl.ds(start, SUBLANES), :]
                for a in range(n_a):
                    if 0 <= m + a < groups:
                        acc[m + a] = acc[m + a] + w_taps[a] * x
        y_ref[c, pl.ds(r0, sub_rows), :] = jnp.concatenate(acc, axis=0)
        return carry

    lax.fori_loop(0, (tr // sub_rows) * n_col, piece, 0)

    for r0 in range(0, tr, 32):
        y = jnp.concatenate([y_ref[c, r0:r0 + 32, :] for c in range(n_col)], axis=1)
        y = _ln_swish(y + b_ref[...], g_ref[...], bl_ref[...])
        o_ref[r0:r0 + 32, :] = y.astype(BF16)


def _conv_prompt(u, w_dw, b_dw, g_ln, b_ln, batch, seq, tr):
    n_t = seq // tr
    sub_rows, sub_cols = 128, 128
    n_col = CONV_CH // sub_cols
    tile = pl.BlockSpec((tr, CONV_CH), lambda b, i: (b * n_t + i, 0))
    vec = _resident((1, CONV_CH))
    return pl.pallas_call(
        functools.partial(_conv_prompt_kernel, sub_rows=sub_rows, sub_cols=sub_cols),
        grid=(batch, n_t),
        in_specs=[tile, _resident((CONV_W, CONV_CH)), vec, vec, vec],
        out_specs=tile,
        out_shape=jax.ShapeDtypeStruct((batch * seq, CONV_CH), BF16),
        scratch_shapes=[pltpu.VMEM((SUBLANES, n_col, CONV_HALO + tr, sub_cols), F32),
                        pltpu.VMEM((CONV_W, n_col, SUBLANES, sub_cols), F32),
                        pltpu.VMEM((n_col, tr, sub_cols), F32)],
        compiler_params=_params(2),
        name="conv_prompt",
    )(u, w_dw, b_dw, g_ln, b_ln)


def _conv_sample_kernel(u_ref, st_ref, w_ref, b_ref, g_ref, bl_ref, o_ref, ns_ref, ext_ref, y_ref,
                        *, dec_seq):
    bb = st_ref.shape[0]
    for b in range(bb):
        new = slice(b * dec_seq, (b + 1) * dec_seq)
        ext_ref[:CONV_BUF, :] = st_ref[b]
        ext_ref[CONV_BUF:CONV_BUF + dec_seq, :] = u_ref[new, :]
        ns_ref[b] = ext_ref[dec_seq:dec_seq + CONV_BUF, :]
        acc = jnp.zeros((dec_seq, CONV_CH), F32)
        for j in range(CONV_W):
            acc = acc + w_ref[j:j + 1, :] * ext_ref[j:j + dec_seq, :]
        y_ref[new, :] = acc
    o_ref[...] = _ln_swish(y_ref[...] + b_ref[...], g_ref[...], bl_ref[...]).astype(BF16)


def _conv_sample(u, state, w_dw, b_dw, g_ln, b_ln, dec_seq, bb):
    dec_batch = state.shape[0]
    rows = bb * dec_seq
    tile = pl.BlockSpec((rows, CONV_CH), lambda i: (i, 0))
    st = pl.BlockSpec((bb, CONV_BUF, CONV_CH), lambda i: (i, 0, 0))
    vec = _resident((1, CONV_CH))
    return pl.pallas_call(
        functools.partial(_conv_sample_kernel, dec_seq=dec_seq),
        grid=(dec_batch // bb,),
        in_specs=[tile, st, _resident((CONV_W, CONV_CH)), vec, vec, vec],
        out_specs=[tile, st],
        out_shape=[jax.ShapeDtypeStruct((dec_batch * dec_seq, CONV_CH), BF16),
                   jax.ShapeDtypeStruct(state.shape, F32)],
        scratch_shapes=[pltpu.VMEM((CONV_BUF + 2 * dec_seq + 8, CONV_CH), F32),
                        pltpu.VMEM((rows, CONV_CH), F32)],
        compiler_params=_params(1),
        name="conv_sample",
    )(u, state, w_dw, b_dw, g_ln, b_ln)


def _mlp_kernel(x_ref, a_ref, c_ref, wo_ref, g_ref, wu_ref, wd_ref, o_ref, h_ref):
    @pl.when(pl.program_id(1) == 0)
    def _():
        mix = jnp.concatenate([a_ref[...], c_ref[...]], axis=1)
        x1 = x_ref[...] + _dot(mix, wo_ref[...])
        o_ref[...] = x1
        h_ref[...] = _rms(x1, g_ref[...]).astype(BF16)

    up = _dot(h_ref[...], wu_ref[...])
    act = jnp.square(jnp.maximum(up, 0.0)).astype(BF16)
    o_ref[...] += _dot(act, wd_ref[...])


def _mlp(x, attn_o, conv_o, w_out, g_mlp, w_up, w_down, tm, tf):
    t = x.shape[0]
    row = lambda w: pl.BlockSpec((tm, w), lambda i, j: (i, 0))
    return pl.pallas_call(
        _mlp_kernel,
        grid=(t // tm, D_FF // tf),
        in_specs=[row(D_MODEL), row(ATTN_W), row(CONV_CH),
                  _resident((D_MODEL, D_MODEL)), _resident((1, D_MODEL)),
                  pl.BlockSpec((D_MODEL, tf), lambda i, j: (0, j)),
                  pl.BlockSpec((tf, D_MODEL), lambda i, j: (j, 0))],
        out_specs=row(D_MODEL),
        out_shape=jax.ShapeDtypeStruct((t, D_MODEL), F32),
        scratch_shapes=[pltpu.VMEM((tm, D_MODEL), BF16)],
        compiler_params=_params(2),
        name="merge_mlp",
    )(x, attn_o, conv_o, w_out, g_mlp, w_up, w_down)


def _tile_heads(gain):
    return jnp.tile(gain.reshape(1, HEAD_DIM), (1, HEADS_PER_CHUNK))


def kernel(x_prompt, x_sample, cache_k, cache_v, state_conv, g_mix_norm, w_in, q_norm, k_norm, sinks,
           w_dw, b_dw, g_conv_ln, b_conv_ln, w_out, g_mlp_norm, w_up, w_down):
    batch, seq, _ = x_prompt.shape
    dec_batch, dec_seq, _ = x_sample.shape
    depth = w_in.shape[0]
    win = cache_k.shape[2]
    assert win == WINDOW == BLOCK and seq % BLOCK == 0 and PAST_LEN >= win

    xp = x_prompt.reshape(batch * seq, D_MODEL)
    xs = x_sample.reshape(dec_batch * dec_seq, D_MODEL)
    tm_p = min(512, batch * seq)
    tm_s = min(512, dec_batch * dec_seq)
    bb = min(8, dec_batch)
    outs = [[] for _ in range(6)]
    for l in range(depth):
        w_in_l, w_out_l = w_in[l].astype(BF16), w_out[l].astype(BF16)
        w_up_l, w_down_l = w_up[l].astype(BF16), w_down[l].astype(BF16)
        g_mix, g_mlp = g_mix_norm[l].reshape(1, D_MODEL), g_mlp_norm[l].reshape(1, D_MODEL)
        qg, kg = _tile_heads(q_norm[l]), _tile_heads(k_norm[l])
        conv_vecs = (w_dw[l], b_dw[l].reshape(1, CONV_CH), g_conv_ln[l].reshape(1, CONV_CH),
                     b_conv_ln[l].reshape(1, CONV_CH))

        q, k, v, u = _proj(xp, g_mix, w_in_l, qg, kg, tm_p)
        attn_o = _attn_prompt(sinks[l], q, k, v, batch, seq)
        conv_o = _conv_prompt(u, *conv_vecs, batch, seq, min(256, seq))
        xp = _mlp(xp, attn_o, conv_o, w_out_l, g_mlp, w_up_l, w_down_l, tm_p, 1024)
        outs[0].append(k.reshape(batch, seq, N_KV_HEADS, HEAD_DIM)[:, seq - win:])
        outs[1].append(v.reshape(batch, seq, N_KV_HEADS, HEAD_DIM)[:, seq - win:])
        outs[2].append(u.reshape(batch, seq, CONV_CH)[:, seq - CONV_BUF:])

        q, k, v, u = _proj(xs, g_mix, w_in_l, qg, kg, tm_s)
        attn_o, nk, nv = _attn_sample(sinks[l], q, k, v,
                                      cache_k[l].reshape(dec_batch, win, KV_W),
                                      cache_v[l].reshape(dec_batch, win, KV_W), dec_seq, bb)
        conv_o, ns = _conv_sample(u, state_conv[l], *conv_vecs, dec_seq, bb)
        xs = _mlp(xs, attn_o, conv_o, w_out_l, g_mlp, w_up_l, w_down_l, tm_s, 1024)
        outs[3].append(nk.reshape(dec_batch, win, N_KV_HEADS, HEAD_DIM))
        outs[4].append(nv.reshape(dec_batch, win, N_KV_HEADS, HEAD_DIM))
        outs[5].append(ns)

    return (xp.reshape(batch, seq, D_MODEL), xs.reshape(dec_batch, dec_seq, D_MODEL),
            *(jnp.stack(o) for o in outs))
```

```python
import functools

import jax
import jax.numpy as jnp
from jax import lax
from jax.experimental import pallas as pl
from jax.experimental.pallas import tpu as pltpu

F32 = jnp.float32
BF16 = jnp.bfloat16

D_MODEL = 2048
HEAD_DIM = 64
ATTN_W = D_MODEL // 2
N_HEADS = ATTN_W // HEAD_DIM
N_KV_HEADS = N_HEADS // 4
KV_W = N_KV_HEADS * HEAD_DIM
CONV_CH = D_MODEL - ATTN_W
IN_W = ATTN_W + 2 * KV_W + 2 * CONV_CH
WINDOW = 128
BLOCK = 128
CONV_W = 31
CONV_BUF = CONV_W - 1
D_FF = 4 * D_MODEL
EPS = 1e-6
ATTN_SCALE = HEAD_DIM ** -0.5
NEG = -1e30
PAST_LEN = 16384

LANES = 128
SUBLANES = 8
HEADS_PER_CHUNK = LANES // HEAD_DIM
Q_CHUNKS = ATTN_W // LANES
KV_CHUNKS = KV_W // LANES
CONV_HALO = 32
VMEM_LIMIT = 56 * 1024 * 1024
MLP_ROWS = 1024
MLP_FF_CHUNK = 512


def _dot(a, b):
    return jnp.dot(a, b, preferred_element_type=F32)


def _dot_nt(a, b):
    return lax.dot_general(a, b, (((1,), (1,)), ((), ())), preferred_element_type=F32)


def _params(n_axes):
    return pltpu.CompilerParams(dimension_semantics=("arbitrary",) * n_axes,
                                vmem_limit_bytes=VMEM_LIMIT)


def _resident(shape):
    return pl.BlockSpec(shape, lambda *_: (0,) * len(shape), pipeline_mode=pl.Buffered(1))


def _rms(x, gain):
    ms = jnp.mean(x * x, axis=-1, keepdims=True)
    return x * lax.rsqrt(ms + EPS) * gain


def _head_rms_chunk(zc, is_lo):
    sq = zc * zc
    sq_lo = jnp.where(is_lo, sq, 0.0)
    sum_lo = jnp.sum(sq_lo, axis=-1, keepdims=True)
    sum_hi = jnp.sum(sq - sq_lo, axis=-1, keepdims=True)
    ms = jnp.where(is_lo, sum_lo, sum_hi) * (1.0 / HEAD_DIM)
    return zc * lax.rsqrt(ms + EPS)


def _proj_kernel(x_ref, g_ref, w_ref, qg_ref, kg_ref, q_ref, k_ref, v_ref, u_ref):
    tm = x_ref.shape[0]
    h = _rms(x_ref[...], g_ref[...]).astype(BF16)
    is_lo = lax.broadcasted_iota(jnp.int32, (tm, LANES), 1) < HEAD_DIM

    zq = _dot(h, w_ref[:, :ATTN_W])
    qg = qg_ref[...] * ATTN_SCALE
    for c in range(Q_CHUNKS):
        sl = slice(c * LANES, (c + 1) * LANES)
        q_ref[:, sl] = (_head_rms_chunk(zq[:, sl], is_lo) * qg).astype(BF16)

    zk = _dot(h, w_ref[:, ATTN_W:ATTN_W + KV_W])
    for c in range(KV_CHUNKS):
        sl = slice(c * LANES, (c + 1) * LANES)
        k_ref[:, sl] = _head_rms_chunk(zk[:, sl], is_lo) * kg_ref[...]

    v_ref[...] = _dot(h, w_ref[:, ATTN_W + KV_W:ATTN_W + 2 * KV_W])

    a0 = ATTN_W + 2 * KV_W
    a = _dot(h, w_ref[:, a0:a0 + CONV_CH])
    gate = _dot(h, w_ref[:, a0 + CONV_CH:])
    u_ref[...] = a * jax.nn.sigmoid(gate)


def _proj(x, g_mix, w_in, qg, kg, tm):
    t = x.shape[0]
    row = lambda w: pl.BlockSpec((tm, w), lambda i: (i, 0))
    return pl.pallas_call(
        _proj_kernel,
        grid=(t // tm,),
        in_specs=[row(D_MODEL), _resident((1, D_MODEL)), _resident((D_MODEL, IN_W)),
                  _resident((1, LANES)), _resident((1, LANES))],
        out_specs=[row(ATTN_W), row(KV_W), row(KV_W), row(CONV_CH)],
        out_shape=[jax.ShapeDtypeStruct((t, ATTN_W), BF16),
                   jax.ShapeDtypeStruct((t, KV_W), F32),
                   jax.ShapeDtypeStruct((t, KV_W), F32),
                   jax.ShapeDtypeStruct((t, CONV_CH), F32)],
        compiler_params=_params(1),
        name="proj",
    )(x, g_mix, w_in, qg, kg)


def _split_heads(chunk, is_lo):
    swapped = pltpu.roll(chunk, HEAD_DIM, axis=1)
    head_a = (jnp.where(is_lo, chunk, 0.0), jnp.where(is_lo, 0.0, swapped))
    head_b = (jnp.where(is_lo, swapped, 0.0), jnp.where(is_lo, 0.0, chunk))
    return head_a, head_b


def _softmax_parts(s_list, sink):
    m = sink
    for s in s_list:
        m = jnp.maximum(m, jnp.max(s, axis=-1, keepdims=True))
    p_list = [jnp.exp(s - m) for s in s_list]
    den = jnp.exp(sink - m)
    for p in p_list:
        den = den + jnp.sum(p, axis=-1, keepdims=True)
    return p_list, 1.0 / den


def _attn_prompt_kernel(sink_ref, q_ref, k_ref, v_ref, o_ref,
                        kk_ref, vx_ref, s_ref, po_ref, pp_ref, es_ref):
    n_blk = q_ref.shape[0] // BLOCK
    is_lo = lax.broadcasted_iota(jnp.int32, (BLOCK, LANES), 1) < HEAD_DIM
    tri = (lax.broadcasted_iota(jnp.int32, (BLOCK, BLOCK), 1)
           <= lax.broadcasted_iota(jnp.int32, (BLOCK, BLOCK), 0))
    tri2 = ((lax.broadcasted_iota(jnp.int32, (2 * BLOCK, 2 * BLOCK), 1) & (BLOCK - 1))
            <= (lax.broadcasted_iota(jnp.int32, (2 * BLOCK, 2 * BLOCK), 0) & (BLOCK - 1)))
    ones_pat = ((lax.broadcasted_iota(jnp.int32, (2 * BLOCK, LANES), 0) < BLOCK)
                == (lax.broadcasted_iota(jnp.int32, (2 * BLOCK, LANES), 1) < HEAD_DIM)).astype(BF16)

    def build(n, carry):
        rows = pl.ds(pl.multiple_of(n * BLOCK, BLOCK), BLOCK)
        for c in range(KV_CHUNKS):
            sl = slice(c * LANES, (c + 1) * LANES)
            k_heads = _split_heads(k_ref[rows, sl], is_lo)
            v_heads = _split_heads(v_ref[rows, sl], is_lo)
            for half in range(HEADS_PER_CHUNK):
                g = HEADS_PER_CHUNK * c + half
                kk_ref[g, n, :BLOCK, :] = k_heads[half][0].astype(BF16)
                kk_ref[g, n, BLOCK:, :] = k_heads[half][1].astype(BF16)
                vx_ref[g, n, :BLOCK, :LANES] = v_heads[half][0].astype(BF16)
                vx_ref[g, n, BLOCK:, :LANES] = v_heads[half][1].astype(BF16)
                vx_ref[g, n, :, LANES:] = ones_pat
        return carry

    lax.fori_loop(0, n_blk, build, 0)

    def block(n, has_prev):
        rows = pl.ds(pl.multiple_of(n * BLOCK, BLOCK), BLOCK)
        for g in range(N_KV_HEADS):
            q2 = q_ref[rows, 2 * g * LANES:2 * (g + 1) * LANES]
            qs = jnp.concatenate([q2[:, :LANES], q2[:, LANES:]], axis=0)
            s_own = _dot_nt(qs, kk_ref[g, n])
            s_ref[g] = jnp.where(tri2, s_own, _dot_nt(qs, kk_ref[g, n - 1]) if has_prev else NEG)
        for g in range(N_KV_HEADS):
            for jj in range(HEADS_PER_CHUNK):
                rs = slice(jj * BLOCK, (jj + 1) * BLOCK)
                sink_terms = []
                for e in range(HEADS_PER_CHUNK):
                    cs = slice(e * BLOCK, (e + 1) * BLOCK)
                    sink = sink_ref[HEADS_PER_CHUNK * (HEADS_PER_CHUNK * g + jj) + e]
                    s = s_ref[g, rs, cs]
                    m = jnp.maximum(jnp.max(s, axis=-1, keepdims=True), sink)
                    p = jnp.exp(s - m)
                    p_own = jnp.where(tri, p, 0.0)
                    po_ref[g, rs, cs] = p_own.astype(BF16)
                    if has_prev:
                        pp_ref[g, rs, cs] = (p - p_own).astype(BF16)
                    sink_terms.append(jnp.exp(sink - m))
                es_ref[g, rs, :] = jnp.where(is_lo, sink_terms[0], sink_terms[1])
        for g in range(N_KV_HEADS):
            ox = _dot(po_ref[g], vx_ref[g, n])
            if has_prev:
                ox = ox + _dot(pp_ref[g], vx_ref[g, n - 1])
            o = ox[:, :LANES] / (ox[:, LANES:] + es_ref[g])
            for jj in range(HEADS_PER_CHUNK):
                j = HEADS_PER_CHUNK * g + jj
                o_ref[rows, j * LANES:(j + 1) * LANES] = o[jj * BLOCK:(jj + 1) * BLOCK].astype(BF16)

    block(0, False)

    def body(n, carry):
        block(n, True)
        return carry

    lax.fori_loop(1, n_blk, body, 0)


def _attn_prompt(sinks, q, k, v, batch, seq):
    n_blk = seq // BLOCK
    row = lambda w: pl.BlockSpec((seq, w), lambda b: (b, 0))
    stage = lambda w, dt: pltpu.VMEM((N_KV_HEADS, 2 * BLOCK, w), dt)
    return pl.pallas_call(
        _attn_prompt_kernel,
        grid=(batch,),
        in_specs=[pl.BlockSpec(memory_space=pltpu.SMEM), row(ATTN_W), row(KV_W), row(KV_W)],
        out_specs=row(ATTN_W),
        out_shape=jax.ShapeDtypeStruct((batch * seq, ATTN_W), BF16),
        scratch_shapes=[pltpu.VMEM((N_KV_HEADS, n_blk, 2 * BLOCK, LANES), BF16),
                        pltpu.VMEM((N_KV_HEADS, n_blk, 2 * BLOCK, 2 * LANES), BF16),
                        stage(2 * BLOCK, F32), stage(2 * BLOCK, BF16), stage(2 * BLOCK, BF16),
                        stage(LANES, F32)],
        compiler_params=_params(1),
        name="attn_prompt",
    )(sinks, q, k, v)


def _attn_sample_kernel(sink_ref, q_ref, k_ref, v_ref, ck_ref, cv_ref,
                        o_ref, nk_ref, nv_ref, kk_ref, vv_ref, *, dec_seq):
    bb = ck_ref.shape[0]
    rows = bb * dec_seq
    win = ck_ref.shape[1]
    is_lo = lax.broadcasted_iota(jnp.int32, (win, LANES), 1) < HEAD_DIM
    is_lo_r = lax.broadcasted_iota(jnp.int32, (rows, LANES), 1) < HEAD_DIM

    for b in range(bb):
        new = slice(b * dec_seq, (b + 1) * dec_seq)
        nk_ref[b, :win - dec_seq, :] = ck_ref[b, dec_seq:, :]
        nk_ref[b, win - dec_seq:, :] = k_ref[new, :]
        nv_ref[b, :win - dec_seq, :] = cv_ref[b, dec_seq:, :]
        nv_ref[b, win - dec_seq:, :] = v_ref[new, :]

    for b in range(bb):
        for c in range(KV_CHUNKS):
            sl = slice(c * LANES, (c + 1) * LANES)
            for src, dst in ((ck_ref, kk_ref), (cv_ref, vv_ref)):
                for half, (lo, hi) in enumerate(_split_heads(src[b, :, sl], is_lo)):
                    g = HEADS_PER_CHUNK * c + half
                    dst[b, g, :win, :] = lo.astype(BF16)
                    dst[b, g, win:, :] = hi.astype(BF16)

    r_id = lax.broadcasted_iota(jnp.int32, (rows, win), 0)
    c_id = lax.broadcasted_iota(jnp.int32, (rows, win), 1)
    win_mask = c_id > r_id % dec_seq
    rn = lax.broadcasted_iota(jnp.int32, (rows, rows), 0)
    cn = lax.broadcasted_iota(jnp.int32, (rows, rows), 1)
    new_mask = (rn // dec_seq == cn // dec_seq) & (cn % dec_seq <= rn % dec_seq)
    seq_of_row = lax.broadcasted_iota(jnp.int32, (rows, LANES), 0) // dec_seq

    for g in range(N_KV_HEADS):
        c, half = divmod(g, HEADS_PER_CHUNK)
        sl_kv = slice(c * LANES, (c + 1) * LANES)
        kn = _split_heads(k_ref[:, sl_kv], is_lo_r)[half]
        vn = _split_heads(v_ref[:, sl_kv], is_lo_r)[half]
        kk_new = jnp.concatenate([kn[0], kn[1]], axis=0).astype(BF16)
        vv_new = jnp.concatenate([vn[0], vn[1]], axis=0).astype(BF16)
        for jj in range(HEADS_PER_CHUNK):
            j = HEADS_PER_CHUNK * g + jj
            sl = slice(j * LANES, (j + 1) * LANES)
            qj = q_ref[:, sl]
            s_new = _dot_nt(qj, kk_new)
            s_win = jnp.zeros((rows, 2 * win), F32)
            for b in range(bb):
                sb = _dot_nt(qj, kk_ref[b, g])
                s_win = jnp.where(jnp.concatenate([seq_of_row, seq_of_row], axis=1) == b, sb, s_win)
            p_win, p_new, inv = [], [], []
            for e in range(HEADS_PER_CHUNK):
                sw = jnp.where(win_mask, s_win[:, e * win:(e + 1) * win], NEG)
                sn = jnp.where(new_mask, s_new[:, e * rows:(e + 1) * rows], NEG)
                (pw, pn), r = _softmax_parts([sw, sn], sink_ref[HEADS_PER_CHUNK * j + e])
                p_win.append(pw.astype(BF16))
                p_new.append(pn.astype(BF16))
                inv.append(r)
            p_win = jnp.concatenate(p_win, axis=1)
            o = _dot(jnp.concatenate(p_new, axis=1), vv_new)
            for b in range(bb):
                o = o + jnp.where(seq_of_row == b, _dot(p_win, vv_ref[b, g]), 0.0)
            o_ref[:, sl] = (o * jnp.where(is_lo_r, inv[0], inv[1])).astype(BF16)


def _attn_sample(sinks, q, k, v, cache_k, cache_v, dec_seq, bb):
    dec_batch, win, _ = cache_k.shape
    rows = bb * dec_seq
    row = lambda w: pl.BlockSpec((rows, w), lambda i: (i, 0))
    cache = pl.BlockSpec((bb, win, KV_W), lambda i: (i, 0, 0))
    return pl.pallas_call(
        functools.partial(_attn_sample_kernel, dec_seq=dec_seq),
        grid=(dec_batch // bb,),
        in_specs=[pl.BlockSpec(memory_space=pltpu.SMEM), row(ATTN_W), row(KV_W), row(KV_W), cache, cache],
        out_specs=[row(ATTN_W), cache, cache],
        out_shape=[jax.ShapeDtypeStruct((dec_batch * dec_seq, ATTN_W), BF16),
                   jax.ShapeDtypeStruct(cache_k.shape, F32),
                   jax.ShapeDtypeStruct(cache_v.shape, F32)],
        scratch_shapes=[pltpu.VMEM((bb, N_KV_HEADS, 2 * win, LANES), BF16),
                        pltpu.VMEM((bb, N_KV_HEADS, 2 * win, LANES), BF16)],
        compiler_params=_params(1),
        name="attn_sample",
    )(sinks, q, k, v, cache_k, cache_v)


def _ln_swish(y, g, b):
    mu = jnp.mean(y, axis=-1, keepdims=True)
    yc = y - mu
    yn = yc * lax.rsqrt(jnp.mean(yc * yc, axis=-1, keepdims=True) + EPS) * g + b
    return yn * jax.nn.sigmoid(yn)


def _conv_prompt_kernel(u_ref, w_ref, b_ref, g_ref, bl_ref, o_ref, sh_ref, wb_ref, y_ref,
                        *, sub_rows, sub_cols):
    tr = u_ref.shape[0]
    n_col = CONV_CH // sub_cols
    i = pl.program_id(1)

    @pl.when(i == 0)
    def _():
        sh_ref[:, :, :CONV_HALO, :] = jnp.zeros((SUBLANES, n_col, CONV_HALO, sub_cols), F32)

    @pl.when(i > 0)
    def _():
        sh_ref[:, :, :CONV_HALO, :] = sh_ref[:, :, tr:tr + CONV_HALO, :]

    @pl.when((pl.program_id(0) == 0) & (i == 0))
    def _():
        for tap in range(CONV_W):
            for c in range(n_col):
                wb_ref[tap, c] = jnp.broadcast_to(w_ref[tap:tap + 1, c * sub_cols:(c + 1) * sub_cols],
                                                  (SUBLANES, sub_cols))

    for c in range(n_col):
        sh_ref[0, c, CONV_HALO:, :] = u_ref[:, c * sub_cols:(c + 1) * sub_cols]
        x = sh_ref[0, c, CONV_HALO - SUBLANES:, :]
        for r in range(1, SUBLANES):
            sh_ref[r, c, CONV_HALO:, :] = pltpu.roll(x, r, axis=0)[SUBLANES:]

    groups = sub_rows // SUBLANES

    def piece(idx, carry):
        rb, c = idx // n_col, idx % n_col
        r0 = pl.multiple_of(rb * sub_rows, sub_rows)
        acc = [jnp.zeros((SUBLANES, sub_cols), F32)] * groups
        for r in range(SUBLANES):
            n_a = (CONV_W - 1 - r) // SUBLANES + 1
            w_taps = [wb_ref[CONV_W - 1 - (SUBLANES * a + r), c] for a in range(n_a)]
            for m in range(1 - n_a, groups):
                start = pl.multiple_of(r0 + (CONV_HALO + SUBLANES * m), SUBLANES)
                x = sh_ref[r, c, pl.ds(start, SUBLANES), :]
                for a in range(n_a):
                    if 0 <= m + a < groups:
                        acc[m + a] = acc[m + a] + w_taps[a] * x
        y_ref[c, pl.ds(r0, sub_rows), :] = jnp.concatenate(acc, axis=0)
        return carry

    lax.fori_loop(0, (tr // sub_rows) * n_col, piece, 0)

    for r0 in range(0, tr, 32):
        y = jnp.concatenate([y_ref[c, r0:r0 + 32, :] for c in range(n_col)], axis=1)
        y = _ln_swish(y + b_ref[...], g_ref[...], bl_ref[...])
        o_ref[r0:r0 + 32, :] = y.astype(BF16)


def _conv_prompt(u, w_dw, b_dw, g_ln, b_ln, batch, seq, tr):
    n_t = seq // tr
    sub_rows, sub_cols = 128, 128
    n_col = CONV_CH // sub_cols
    tile = pl.BlockSpec((tr, CONV_CH), lambda b, i: (b * n_t + i, 0))
    vec = _resident((1, CONV_CH))
    return pl.pallas_call(
        functools.partial(_conv_prompt_kernel, sub_rows=sub_rows, sub_cols=sub_cols),
        grid=(batch, n_t),
        in_specs=[tile, _resident((CONV_W, CONV_CH)), vec, vec, vec],
        out_specs=tile,
        out_shape=jax.ShapeDtypeStruct((batch * seq, CONV_CH), BF16),
        scratch_shapes=[pltpu.VMEM((SUBLANES, n_col, CONV_HALO + tr, sub_cols), F32),
                        pltpu.VMEM((CONV_W, n_col, SUBLANES, sub_cols), F32),
                        pltpu.VMEM((n_col, tr, sub_cols), F32)],
        compiler_params=_params(2),
        name="conv_prompt",
    )(u, w_dw, b_dw, g_ln, b_ln)


def _conv_sample_kernel(u_ref, st_ref, w_ref, b_ref, g_ref, bl_ref, o_ref, ns_ref, ext_ref, y_ref,
                        *, dec_seq):
    bb = st_ref.shape[0]
    for b in range(bb):
        new = slice(b * dec_seq, (b + 1) * dec_seq)
        ext_ref[:CONV_BUF, :] = st_ref[b]
        ext_ref[CONV_BUF:CONV_BUF + dec_seq, :] = u_ref[new, :]
        ns_ref[b] = ext_ref[dec_seq:dec_seq + CONV_BUF, :]
        acc = jnp.zeros((dec_seq, CONV_CH), F32)
        for j in range(CONV_W):
            acc = acc + w_ref[j:j + 1, :] * ext_ref[j:j + dec_seq, :]
        y_ref[new, :] = acc
    o_ref[...] = _ln_swish(y_ref[...] + b_ref[...], g_ref[...], bl_ref[...]).astype(BF16)


def _conv_sample(u, state, w_dw, b_dw, g_ln, b_ln, dec_seq, bb):
    dec_batch = state.shape[0]
    rows = bb * dec_seq
    tile = pl.BlockSpec((rows, CONV_CH), lambda i: (i, 0))
    st = pl.BlockSpec((bb, CONV_BUF, CONV_CH), lambda i: (i, 0, 0))
    vec = _resident((1, CONV_CH))
    return pl.pallas_call(
        functools.partial(_conv_sample_kernel, dec_seq=dec_seq),
        grid=(dec_batch // bb,),
        in_specs=[tile, st, _resident((CONV_W, CONV_CH)), vec, vec, vec],
        out_specs=[tile, st],
        out_shape=[jax.ShapeDtypeStruct((dec_batch * dec_seq, CONV_CH), BF16),
                   jax.ShapeDtypeStruct(state.shape, F32)],
        scratch_shapes=[pltpu.VMEM((CONV_BUF + 2 * dec_seq + 8, CONV_CH), F32),
                        pltpu.VMEM((rows, CONV_CH), F32)],
        compiler_params=_params(1),
        name="conv_sample",
    )(u, state, w_dw, b_dw, g_ln, b_ln)


def _merge_kernel(x_ref, a_ref, c_ref, wo_ref, g_ref, x1_ref, h_ref):
    mix = jnp.concatenate([a_ref[...], c_ref[...]], axis=1)
    x1 = x_ref[...] + _dot(mix, wo_ref[...])
    x1_ref[...] = x1
    h_ref[...] = _rms(x1, g_ref[...]).astype(BF16)


def _merge(x, attn_o, conv_o, w_out, g_mlp, tm):
    t = x.shape[0]
    row = lambda w: pl.BlockSpec((tm, w), lambda i: (i, 0))
    return pl.pallas_call(
        _merge_kernel,
        grid=(t // tm,),
        in_specs=[row(D_MODEL), row(ATTN_W), row(CONV_CH),
                  _resident((D_MODEL, D_MODEL)), _resident((1, D_MODEL))],
        out_specs=[row(D_MODEL), row(D_MODEL)],
        out_shape=[jax.ShapeDtypeStruct((t, D_MODEL), F32), jax.ShapeDtypeStruct((t, D_MODEL), BF16)],
        compiler_params=_params(1),
        name="merge",
    )(x, attn_o, conv_o, w_out, g_mlp)


def _mlp_kernel(x1_ref, h_ref, wu_ref, wd_ref, o_ref):
    @pl.when(pl.program_id(1) == 0)
    def _():
        o_ref[...] = x1_ref[...]

    up = _dot(h_ref[...], wu_ref[...])
    act = jnp.square(jnp.maximum(up, 0.0)).astype(BF16)
    o_ref[...] += _dot(act, wd_ref[...])


def _mlp(x1, h, w_up_chunks, w_down, tm):
    t = x1.shape[0]
    n_ff, _, tf = w_up_chunks.shape
    row = lambda w: pl.BlockSpec((tm, w), lambda i, j: (i, 0))
    return pl.pallas_call(
        _mlp_kernel,
        grid=(t // tm, n_ff),
        in_specs=[row(D_MODEL), row(D_MODEL),
                  pl.BlockSpec((None, D_MODEL, tf), lambda i, j: (j, 0, 0)),
                  pl.BlockSpec((tf, D_MODEL), lambda i, j: (j, 0))],
        out_specs=row(D_MODEL),
        out_shape=jax.ShapeDtypeStruct((t, D_MODEL), F32),
        compiler_params=_params(2),
        name="mlp",
    )(x1, h, w_up_chunks, w_down)


def _tile_heads(gain):
    return jnp.tile(gain.reshape(1, HEAD_DIM), (1, HEADS_PER_CHUNK))


def kernel(x_prompt, x_sample, cache_k, cache_v, state_conv, g_mix_norm, w_in, q_norm, k_norm, sinks,
           w_dw, b_dw, g_conv_ln, b_conv_ln, w_out, g_mlp_norm, w_up, w_down):
    batch, seq, _ = x_prompt.shape
    dec_batch, dec_seq, _ = x_sample.shape
    depth = w_in.shape[0]
    win = cache_k.shape[2]
    assert win == WINDOW == BLOCK and seq % BLOCK == 0 and PAST_LEN >= win

    xp = x_prompt.reshape(batch * seq, D_MODEL)
    xs = x_sample.reshape(dec_batch * dec_seq, D_MODEL)
    tm_p = min(512, batch * seq)
    tm_s = min(512, dec_batch * dec_seq)
    tm_mlp_p = min(MLP_ROWS, batch * seq)
    bb = min(8, dec_batch)
    outs = [[] for _ in range(6)]
    for l in range(depth):
        w_in_l, w_out_l = w_in[l].astype(BF16), w_out[l].astype(BF16)
        w_up_l = w_up[l].astype(BF16).reshape(D_MODEL, D_FF // MLP_FF_CHUNK, MLP_FF_CHUNK).transpose(1, 0, 2)
        w_down_l = w_down[l].astype(BF16)
        g_mix, g_mlp = g_mix_norm[l].reshape(1, D_MODEL), g_mlp_norm[l].reshape(1, D_MODEL)
        qg, kg = _tile_heads(q_norm[l]), _tile_heads(k_norm[l])
        conv_vecs = (w_dw[l], b_dw[l].reshape(1, CONV_CH), g_conv_ln[l].reshape(1, CONV_CH),
                     b_conv_ln[l].reshape(1, CONV_CH))

        q, k, v, u = _proj(xp, g_mix, w_in_l, qg, kg, tm_p)
        attn_o = _attn_prompt(sinks[l], q, k, v, batch, seq)
        conv_o = _conv_prompt(u, *conv_vecs, batch, seq, min(512, seq))
        x1, h = _merge(xp, attn_o, conv_o, w_out_l, g_mlp, tm_p)
        xp = _mlp(x1, h, w_up_l, w_down_l, tm_mlp_p)
        outs[0].append(k.reshape(batch, seq, N_KV_HEADS, HEAD_DIM)[:, seq - win:])
        outs[1].append(v.reshape(batch, seq, N_KV_HEADS, HEAD_DIM)[:, seq - win:])
        outs[2].append(u.reshape(batch, seq, CONV_CH)[:, seq - CONV_BUF:])

        q, k, v, u = _proj(xs, g_mix, w_in_l, qg, kg, tm_s)
        attn_o, nk, nv = _attn_sample(sinks[l], q, k, v,
                                      cache_k[l].reshape(dec_batch, win, KV_W),
                                      cache_v[l].reshape(dec_batch, win, KV_W), dec_seq, bb)
        conv_o, ns = _conv_sample(u, state_conv[l], *conv_vecs, dec_seq, bb)
        x1, h = _merge(xs, attn_o, conv_o, w_out_l, g_mlp, tm_s)
        xs = _mlp(x1, h, w_up_l, w_down_l, tm_s)
        outs[3].append(nk.reshape(dec_batch, win, N_KV_HEADS, HEAD_DIM))
        outs[4].append(nv.reshape(dec_batch, win, N_KV_HEADS, HEAD_DIM))
        outs[5].append(ns)

    return (xp.reshape(batch, seq, D_MODEL), xs.reshape(dec_batch, dec_seq, D_MODEL),
            *(jnp.stack(o) for o in outs))
```

```python
import functools

import jax
import jax.numpy as jnp
from jax import lax
from jax.experimental import pallas as pl
from jax.experimental.pallas import tpu as pltpu

F32 = jnp.float32
BF16 = jnp.bfloat16

D_MODEL = 2048
HEAD_DIM = 64
ATTN_W = D_MODEL // 2
N_HEADS = ATTN_W // HEAD_DIM
N_KV_HEADS = N_HEADS // 4
KV_W = N_KV_HEADS * HEAD_DIM
CONV_CH = D_MODEL - ATTN_W
IN_W = ATTN_W + 2 * KV_W + 2 * CONV_CH
WINDOW = 128
BLOCK = 128
CONV_W = 31
CONV_BUF = CONV_W - 1
D_FF = 4 * D_MODEL
EPS = 1e-6
ATTN_SCALE = HEAD_DIM ** -0.5
NEG = -1e30
PAST_LEN = 16384

LANES = 128
SUBLANES = 8
HEADS_PER_CHUNK = LANES // HEAD_DIM
Q_CHUNKS = ATTN_W // LANES
KV_CHUNKS = KV_W // LANES
CONV_HALO = 32
VMEM_LIMIT = 56 * 1024 * 1024
MLP_ROWS = 1024
MLP_FF_CHUNK = 512
MERGE_ROWS = 256


def _dot(a, b):
    return jnp.dot(a, b, preferred_element_type=F32)


def _dot_nt(a, b):
    return lax.dot_general(a, b, (((1,), (1,)), ((), ())), preferred_element_type=F32)


def _params(n_axes):
    return pltpu.CompilerParams(dimension_semantics=("arbitrary",) * n_axes,
                                vmem_limit_bytes=VMEM_LIMIT)


def _resident(shape):
    return pl.BlockSpec(shape, lambda *_: (0,) * len(shape), pipeline_mode=pl.Buffered(1))


def _rms(x, gain):
    ms = jnp.mean(x * x, axis=-1, keepdims=True)
    return x * lax.rsqrt(ms + EPS) * gain


def _head_rms_chunk(zc, is_lo):
    sq = zc * zc
    sq_lo = jnp.where(is_lo, sq, 0.0)
    sum_lo = jnp.sum(sq_lo, axis=-1, keepdims=True)
    sum_hi = jnp.sum(sq - sq_lo, axis=-1, keepdims=True)
    ms = jnp.where(is_lo, sum_lo, sum_hi) * (1.0 / HEAD_DIM)
    return zc * lax.rsqrt(ms + EPS)


def _proj_kernel(x_ref, g_ref, w_ref, qg_ref, kg_ref, q_ref, k_ref, v_ref, u_ref):
    tm = x_ref.shape[0]
    h = _rms(x_ref[...], g_ref[...]).astype(BF16)
    is_lo = lax.broadcasted_iota(jnp.int32, (tm, LANES), 1) < HEAD_DIM

    zq = _dot(h, w_ref[:, :ATTN_W])
    qg = qg_ref[...] * ATTN_SCALE
    for c in range(Q_CHUNKS):
        sl = slice(c * LANES, (c + 1) * LANES)
        q_ref[:, sl] = (_head_rms_chunk(zq[:, sl], is_lo) * qg).astype(BF16)

    zk = _dot(h, w_ref[:, ATTN_W:ATTN_W + KV_W])
    for c in range(KV_CHUNKS):
        sl = slice(c * LANES, (c + 1) * LANES)
        k_ref[:, sl] = _head_rms_chunk(zk[:, sl], is_lo) * kg_ref[...]

    v_ref[...] = _dot(h, w_ref[:, ATTN_W + KV_W:ATTN_W + 2 * KV_W])

    a0 = ATTN_W + 2 * KV_W
    a = _dot(h, w_ref[:, a0:a0 + CONV_CH])
    gate = _dot(h, w_ref[:, a0 + CONV_CH:])
    u_ref[...] = a * jax.nn.sigmoid(gate)


def _proj(x, g_mix, w_in, qg, kg, tm):
    t = x.shape[0]
    row = lambda w: pl.BlockSpec((tm, w), lambda i: (i, 0))
    return pl.pallas_call(
        _proj_kernel,
        grid=(t // tm,),
        in_specs=[row(D_MODEL), _resident((1, D_MODEL)), _resident((D_MODEL, IN_W)),
                  _resident((1, LANES)), _resident((1, LANES))],
        out_specs=[row(ATTN_W), row(KV_W), row(KV_W), row(CONV_CH)],
        out_shape=[jax.ShapeDtypeStruct((t, ATTN_W), BF16),
                   jax.ShapeDtypeStruct((t, KV_W), F32),
                   jax.ShapeDtypeStruct((t, KV_W), F32),
                   jax.ShapeDtypeStruct((t, CONV_CH), F32)],
        compiler_params=_params(1),
        name="proj",
    )(x, g_mix, w_in, qg, kg)


def _split_heads(chunk, is_lo):
    swapped = pltpu.roll(chunk, HEAD_DIM, axis=1)
    head_a = (jnp.where(is_lo, chunk, 0.0), jnp.where(is_lo, 0.0, swapped))
    head_b = (jnp.where(is_lo, swapped, 0.0), jnp.where(is_lo, 0.0, chunk))
    return head_a, head_b


def _softmax_parts(s_list, sink):
    m = sink
    for s in s_list:
        m = jnp.maximum(m, jnp.max(s, axis=-1, keepdims=True))
    p_list = [jnp.exp(s - m) for s in s_list]
    den = jnp.exp(sink - m)
    for p in p_list:
        den = den + jnp.sum(p, axis=-1, keepdims=True)
    return p_list, 1.0 / den


def _attn_prompt_kernel(sink_ref, q_ref, k_ref, v_ref, o_ref,
                        kk_ref, vx_ref, s_ref, po_ref, pp_ref, es_ref):
    n_blk = q_ref.shape[0] // BLOCK
    is_lo = lax.broadcasted_iota(jnp.int32, (BLOCK, LANES), 1) < HEAD_DIM
    tri = (lax.broadcasted_iota(jnp.int32, (BLOCK, BLOCK), 1)
           <= lax.broadcasted_iota(jnp.int32, (BLOCK, BLOCK), 0))
    tri2 = ((lax.broadcasted_iota(jnp.int32, (2 * BLOCK, 2 * BLOCK), 1) & (BLOCK - 1))
            <= (lax.broadcasted_iota(jnp.int32, (2 * BLOCK, 2 * BLOCK), 0) & (BLOCK - 1)))
    ones_pat = ((lax.broadcasted_iota(jnp.int32, (2 * BLOCK, LANES), 0) < BLOCK)
                == (lax.broadcasted_iota(jnp.int32, (2 * BLOCK, LANES), 1) < HEAD_DIM)).astype(BF16)

    def build(n, carry):
        rows = pl.ds(pl.multiple_of(n * BLOCK, BLOCK), BLOCK)
        for c in range(KV_CHUNKS):
            sl = slice(c * LANES, (c + 1) * LANES)
            k_heads = _split_heads(k_ref[rows, sl], is_lo)
            v_heads = _split_heads(v_ref[rows, sl], is_lo)
            for half in range(HEADS_PER_CHUNK):
                g = HEADS_PER_CHUNK * c + half
                kk_ref[g, n, :BLOCK, :] = k_heads[half][0].astype(BF16)
                kk_ref[g, n, BLOCK:, :] = k_heads[half][1].astype(BF16)
                vx_ref[g, n, :BLOCK, :LANES] = v_heads[half][0].astype(BF16)
                vx_ref[g, n, BLOCK:, :LANES] = v_heads[half][1].astype(BF16)
                vx_ref[g, n, :, LANES:] = ones_pat
        return carry

    lax.fori_loop(0, n_blk, build, 0)

    def block(n, has_prev):
        rows = pl.ds(pl.multiple_of(n * BLOCK, BLOCK), BLOCK)
        for g in range(N_KV_HEADS):
            q2 = q_ref[rows, 2 * g * LANES:2 * (g + 1) * LANES]
            qs = jnp.concatenate([q2[:, :LANES], q2[:, LANES:]], axis=0)
            s_own = _dot_nt(qs, kk_ref[g, n])
            s_ref[g] = jnp.where(tri2, s_own, _dot_nt(qs, kk_ref[g, n - 1]) if has_prev else NEG)
        for g in range(N_KV_HEADS):
            for jj in range(HEADS_PER_CHUNK):
                rs = slice(jj * BLOCK, (jj + 1) * BLOCK)
                sink_terms = []
                for e in range(HEADS_PER_CHUNK):
                    cs = slice(e * BLOCK, (e + 1) * BLOCK)
                    sink = sink_ref[HEADS_PER_CHUNK * (HEADS_PER_CHUNK * g + jj) + e]
                    s = s_ref[g, rs, cs]
                    m = jnp.maximum(jnp.max(s, axis=-1, keepdims=True), sink)
                    p = jnp.exp(s - m)
                    p_own = jnp.where(tri, p, 0.0)
                    po_ref[g, rs, cs] = p_own.astype(BF16)
                    if has_prev:
                        pp_ref[g, rs, cs] = (p - p_own).astype(BF16)
                    sink_terms.append(jnp.exp(sink - m))
                es_ref[g, rs, :] = jnp.where(is_lo, sink_terms[0], sink_terms[1])
        for g in range(N_KV_HEADS):
            ox = _dot(po_ref[g], vx_ref[g, n])
            if has_prev:
                ox = ox + _dot(pp_ref[g], vx_ref[g, n - 1])
            o = ox[:, :LANES] / (ox[:, LANES:] + es_ref[g])
            for jj in range(HEADS_PER_CHUNK):
                j = HEADS_PER_CHUNK * g + jj
                o_ref[rows, j * LANES:(j + 1) * LANES] = o[jj * BLOCK:(jj + 1) * BLOCK].astype(BF16)

    block(0, False)

    def body(n, carry):
        block(n, True)
        return carry

    lax.fori_loop(1, n_blk, body, 0)


def _attn_prompt(sinks, q, k, v, batch, seq):
    n_blk = seq // BLOCK
    row = lambda w: pl.BlockSpec((seq, w), lambda b: (b, 0))
    stage = lambda w, dt: pltpu.VMEM((N_KV_HEADS, 2 * BLOCK, w), dt)
    return pl.pallas_call(
        _attn_prompt_kernel,
        grid=(batch,),
        in_specs=[pl.BlockSpec(memory_space=pltpu.SMEM), row(ATTN_W), row(KV_W), row(KV_W)],
        out_specs=row(ATTN_W),
        out_shape=jax.ShapeDtypeStruct((batch * seq, ATTN_W), BF16),
        scratch_shapes=[pltpu.VMEM((N_KV_HEADS, n_blk, 2 * BLOCK, LANES), BF16),
                        pltpu.VMEM((N_KV_HEADS, n_blk, 2 * BLOCK, 2 * LANES), BF16),
                        stage(2 * BLOCK, F32), stage(2 * BLOCK, BF16), stage(2 * BLOCK, BF16),
                        stage(LANES, F32)],
        compiler_params=_params(1),
        name="attn_prompt",
    )(sinks, q, k, v)


def _attn_sample_kernel(sink_ref, q_ref, k_ref, v_ref, ck_ref, cv_ref,
                        o_ref, nk_ref, nv_ref, kk_ref, vv_ref, *, dec_seq):
    bb = ck_ref.shape[0]
    rows = bb * dec_seq
    win = ck_ref.shape[1]
    is_lo = lax.broadcasted_iota(jnp.int32, (win, LANES), 1) < HEAD_DIM
    is_lo_r = lax.broadcasted_iota(jnp.int32, (rows, LANES), 1) < HEAD_DIM

    for b in range(bb):
        new = slice(b * dec_seq, (b + 1) * dec_seq)
        nk_ref[b, :win - dec_seq, :] = ck_ref[b, dec_seq:, :]
        nk_ref[b, win - dec_seq:, :] = k_ref[new, :]
        nv_ref[b, :win - dec_seq, :] = cv_ref[b, dec_seq:, :]
        nv_ref[b, win - dec_seq:, :] = v_ref[new, :]

    for b in range(bb):
        for c in range(KV_CHUNKS):
            sl = slice(c * LANES, (c + 1) * LANES)
            for src, dst in ((ck_ref, kk_ref), (cv_ref, vv_ref)):
                for half, (lo, hi) in enumerate(_split_heads(src[b, :, sl], is_lo)):
                    g = HEADS_PER_CHUNK * c + half
                    dst[b, g, :win, :] = lo.astype(BF16)
                    dst[b, g, win:, :] = hi.astype(BF16)

    r_id = lax.broadcasted_iota(jnp.int32, (rows, win), 0)
    c_id = lax.broadcasted_iota(jnp.int32, (rows, win), 1)
    win_mask = c_id > r_id % dec_seq
    rn = lax.broadcasted_iota(jnp.int32, (rows, rows), 0)
    cn = lax.broadcasted_iota(jnp.int32, (rows, rows), 1)
    new_mask = (rn // dec_seq == cn // dec_seq) & (cn % dec_seq <= rn % dec_seq)
    seq_of_row = lax.broadcasted_iota(jnp.int32, (rows, LANES), 0) // dec_seq

    for g in range(N_KV_HEADS):
        c, half = divmod(g, HEADS_PER_CHUNK)
        sl_kv = slice(c * LANES, (c + 1) * LANES)
        kn = _split_heads(k_ref[:, sl_kv], is_lo_r)[half]
        vn = _split_heads(v_ref[:, sl_kv], is_lo_r)[half]
        kk_new = jnp.concatenate([kn[0], kn[1]], axis=0).astype(BF16)
        vv_new = jnp.concatenate([vn[0], vn[1]], axis=0).astype(BF16)
        for jj in range(HEADS_PER_CHUNK):
            j = HEADS_PER_CHUNK * g + jj
            sl = slice(j * LANES, (j + 1) * LANES)
            qj = q_ref[:, sl]
            s_new = _dot_nt(qj, kk_new)
            s_win = jnp.zeros((rows, 2 * win), F32)
            for b in range(bb):
                sb = _dot_nt(qj, kk_ref[b, g])
                s_win = jnp.where(jnp.concatenate([seq_of_row, seq_of_row], axis=1) == b, sb, s_win)
            p_win, p_new, inv = [], [], []
            for e in range(HEADS_PER_CHUNK):
                sw = jnp.where(win_mask, s_win[:, e * win:(e + 1) * win], NEG)
                sn = jnp.where(new_mask, s_new[:, e * rows:(e + 1) * rows], NEG)
                (pw, pn), r = _softmax_parts([sw, sn], sink_ref[HEADS_PER_CHUNK * j + e])
                p_win.append(pw.astype(BF16))
                p_new.append(pn.astype(BF16))
                inv.append(r)
            p_win = jnp.concatenate(p_win, axis=1)
            o = _dot(jnp.concatenate(p_new, axis=1), vv_new)
            for b in range(bb):
                o = o + jnp.where(seq_of_row == b, _dot(p_win, vv_ref[b, g]), 0.0)
            o_ref[:, sl] = (o * jnp.where(is_lo_r, inv[0], inv[1])).astype(BF16)


def _attn_sample(sinks, q, k, v, cache_k, cache_v, dec_seq, bb):
    dec_batch, win, _ = cache_k.shape
    rows = bb * dec_seq
    row = lambda w: pl.BlockSpec((rows, w), lambda i: (i, 0))
    cache = pl.BlockSpec((bb, win, KV_W), lambda i: (i, 0, 0))
    return pl.pallas_call(
        functools.partial(_attn_sample_kernel, dec_seq=dec_seq),
        grid=(dec_batch // bb,),
        in_specs=[pl.BlockSpec(memory_space=pltpu.SMEM), row(ATTN_W), row(KV_W), row(KV_W), cache, cache],
        out_specs=[row(ATTN_W), cache, cache],
        out_shape=[jax.ShapeDtypeStruct((dec_batch * dec_seq, ATTN_W), BF16),
                   jax.ShapeDtypeStruct(cache_k.shape, F32),
                   jax.ShapeDtypeStruct(cache_v.shape, F32)],
        scratch_shapes=[pltpu.VMEM((bb, N_KV_HEADS, 2 * win, LANES), BF16),
                        pltpu.VMEM((bb, N_KV_HEADS, 2 * win, LANES), BF16)],
        compiler_params=_params(1),
        name="attn_sample",
    )(sinks, q, k, v, cache_k, cache_v)


def _ln_swish(y, g, b):
    mu = jnp.mean(y, axis=-1, keepdims=True)
    yc = y - mu
    yn = yc * lax.rsqrt(jnp.mean(yc * yc, axis=-1, keepdims=True) + EPS) * g + b
    return yn * jax.nn.sigmoid(yn)


def _conv_prompt_kernel(u_ref, w_ref, b_ref, g_ref, bl_ref, o_ref, sh_ref, wb_ref, y_ref,
                        *, sub_rows, sub_cols):
    tr = u_ref.shape[0]
    n_col = CONV_CH // sub_cols
    i = pl.program_id(1)

    @pl.when(i == 0)
    def _():
        sh_ref[:, :, :CONV_HALO, :] = jnp.zeros((SUBLANES, n_col, CONV_HALO, sub_cols), F32)

    @pl.when(i > 0)
    def _():
        sh_ref[:, :, :CONV_HALO, :] = sh_ref[:, :, tr:tr + CONV_HALO, :]

    @pl.when((pl.program_id(0) == 0) & (i == 0))
    def _():
        for tap in range(CONV_W):
            for c in range(n_col):
                wb_ref[tap, c] = jnp.broadcast_to(w_ref[tap:tap + 1, c * sub_cols:(c + 1) * sub_cols],
                                                  (SUBLANES, sub_cols))

    for c in range(n_col):
        sh_ref[0, c, CONV_HALO:, :] = u_ref[:, c * sub_cols:(c + 1) * sub_cols]
        x = sh_ref[0, c, CONV_HALO - SUBLANES:, :]
        for r in range(1, SUBLANES):
            sh_ref[r, c, CONV_HALO:, :] = pltpu.roll(x, r, axis=0)[SUBLANES:]

    groups = sub_rows // SUBLANES

    def piece(idx, carry):
        rb, c = idx // n_col, idx % n_col
        r0 = pl.multiple_of(rb * sub_rows, sub_rows)
        acc = [jnp.zeros((SUBLANES, sub_cols), F32)] * groups
        for r in range(SUBLANES):
            n_a = (CONV_W - 1 - r) // SUBLANES + 1
            w_taps = [wb_ref[CONV_W - 1 - (SUBLANES * a + r), c] for a in range(n_a)]
            for m in range(1 - n_a, groups):
                start = pl.multiple_of(r0 + (CONV_HALO + SUBLANES * m), SUBLANES)
                x = sh_ref[r, c, pl.ds(start, SUBLANES), :]
                for a in range(n_a):
                    if 0 <= m + a < groups:
                        acc[m + a] = acc[m + a] + w_taps[a] * x
        y_ref[c, pl.ds(r0, sub_rows), :] = jnp.concatenate(acc, axis=0)
        return carry

    lax.fori_loop(0, (tr // sub_rows) * n_col, piece, 0)

    for r0 in range(0, tr, 32):
        y = jnp.concatenate([y_ref[c, r0:r0 + 32, :] for c in range(n_col)], axis=1)
        y = _ln_swish(y + b_ref[...], g_ref[...], bl_ref[...])
        o_ref[r0:r0 + 32, :] = y.astype(BF16)


def _conv_prompt(u, w_dw, b_dw, g_ln, b_ln, batch, seq, tr):
    n_t = seq // tr
    sub_rows, sub_cols = 128, 128
    n_col = CONV_CH // sub_cols
    tile = pl.BlockSpec((tr, CONV_CH), lambda b, i: (b * n_t + i, 0))
    vec = _resident((1, CONV_CH))
    return pl.pallas_call(
        functools.partial(_conv_prompt_kernel, sub_rows=sub_rows, sub_cols=sub_cols),
        grid=(batch, n_t),
        in_specs=[tile, _resident((CONV_W, CONV_CH)), vec, vec, vec],
        out_specs=tile,
        out_shape=jax.ShapeDtypeStruct((batch * seq, CONV_CH), BF16),
        scratch_shapes=[pltpu.VMEM((SUBLANES, n_col, CONV_HALO + tr, sub_cols), F32),
                        pltpu.VMEM((CONV_W, n_col, SUBLANES, sub_cols), F32),
                        pltpu.VMEM((n_col, tr, sub_cols), F32)],
        compiler_params=_params(2),
        name="conv_prompt",
    )(u, w_dw, b_dw, g_ln, b_ln)


def _conv_sample_kernel(u_ref, st_ref, w_ref, b_ref, g_ref, bl_ref, o_ref, ns_ref, ext_ref, y_ref,
                        *, dec_seq):
    bb = st_ref.shape[0]
    for b in range(bb):
        new = slice(b * dec_seq, (b + 1) * dec_seq)
        ext_ref[:CONV_BUF, :] = st_ref[b]
        ext_ref[CONV_BUF:CONV_BUF + dec_seq, :] = u_ref[new, :]
        ns_ref[b] = ext_ref[dec_seq:dec_seq + CONV_BUF, :]
        acc = jnp.zeros((dec_seq, CONV_CH), F32)
        for j in range(CONV_W):
            acc = acc + w_ref[j:j + 1, :] * ext_ref[j:j + dec_seq, :]
        y_ref[new, :] = acc
    o_ref[...] = _ln_swish(y_ref[...] + b_ref[...], g_ref[...], bl_ref[...]).astype(BF16)


def _conv_sample(u, state, w_dw, b_dw, g_ln, b_ln, dec_seq, bb):
    dec_batch = state.shape[0]
    rows = bb * dec_seq
    tile = pl.BlockSpec((rows, CONV_CH), lambda i: (i, 0))
    st = pl.BlockSpec((bb, CONV_BUF, CONV_CH), lambda i: (i, 0, 0))
    vec = _resident((1, CONV_CH))
    return pl.pallas_call(
        functools.partial(_conv_sample_kernel, dec_seq=dec_seq),
        grid=(dec_batch // bb,),
        in_specs=[tile, st, _resident((CONV_W, CONV_CH)), vec, vec, vec],
        out_specs=[tile, st],
        out_shape=[jax.ShapeDtypeStruct((dec_batch * dec_seq, CONV_CH), BF16),
                   jax.ShapeDtypeStruct(state.shape, F32)],
        scratch_shapes=[pltpu.VMEM((CONV_BUF + 2 * dec_seq + 8, CONV_CH), F32),
                        pltpu.VMEM((rows, CONV_CH), F32)],
        compiler_params=_params(1),
        name="conv_sample",
    )(u, state, w_dw, b_dw, g_ln, b_ln)


def _merge_mlp_kernel(x_hbm, a_hbm, c_hbm, wo_ref, g_ref, wu_ref, wd_ref, o_ref,
                      x_buf, a_buf, c_buf, h_ref, sem, *, merge_rows):
    i, j = pl.program_id(0), pl.program_id(1)
    tm = o_ref.shape[0]

    def row_copies(tile):
        rows = pl.ds(pl.multiple_of(tile * tm, tm), tm)
        return (pltpu.make_async_copy(x_hbm.at[rows], x_buf, sem.at[0]),
                pltpu.make_async_copy(a_hbm.at[rows], a_buf, sem.at[1]),
                pltpu.make_async_copy(c_hbm.at[rows], c_buf, sem.at[2]))

    @pl.when((i == 0) & (j == 0))
    def _():
        for cp in row_copies(0):
            cp.start()

    @pl.when(j == 0)
    def _():
        for cp in row_copies(i):
            cp.wait()
        for r0 in range(0, tm, merge_rows):
            rs = slice(r0, r0 + merge_rows)
            mix = jnp.concatenate([a_buf[rs, :], c_buf[rs, :]], axis=1)
            x1 = x_buf[rs, :] + _dot(mix, wo_ref[...])
            o_ref[rs, :] = x1
            h_ref[rs, :] = _rms(x1, g_ref[...]).astype(BF16)

    @pl.when((j == 1) & (i + 1 < pl.num_programs(0)))
    def _():
        for cp in row_copies(i + 1):
            cp.start()

    up = _dot(h_ref[...], wu_ref[...])
    act = jnp.square(jnp.maximum(up, 0.0)).astype(BF16)
    o_ref[...] += _dot(act, wd_ref[...])


def _merge_mlp(x, attn_o, conv_o, w_out, g_mlp, w_up, w_down, tm):
    t = x.shape[0]
    n_ff = D_FF // MLP_FF_CHUNK
    merge_rows = min(MERGE_ROWS, tm)
    assert t % tm == 0 and tm % merge_rows == 0 and n_ff >= 2
    hbm = pl.BlockSpec(memory_space=pl.ANY)
    return pl.pallas_call(
        functools.partial(_merge_mlp_kernel, merge_rows=merge_rows),
        grid=(t // tm, n_ff),
        in_specs=[hbm, hbm, hbm, _resident((D_MODEL, D_MODEL)), _resident((1, D_MODEL)),
                  pl.BlockSpec((D_MODEL, MLP_FF_CHUNK), lambda i, j: (0, j)),
                  pl.BlockSpec((MLP_FF_CHUNK, D_MODEL), lambda i, j: (j, 0))],
        out_specs=pl.BlockSpec((tm, D_MODEL), lambda i, j: (i, 0)),
        out_shape=jax.ShapeDtypeStruct((t, D_MODEL), F32),
        scratch_shapes=[pltpu.VMEM((tm, D_MODEL), F32), pltpu.VMEM((tm, ATTN_W), BF16),
                        pltpu.VMEM((tm, CONV_CH), BF16), pltpu.VMEM((tm, D_MODEL), BF16),
                        pltpu.SemaphoreType.DMA((3,))],
        compiler_params=_params(2),
        name="merge_mlp",
    )(x, attn_o, conv_o, w_out, g_mlp, w_up, w_down)


def _tile_heads(gain):
    return jnp.tile(gain.reshape(1, HEAD_DIM), (1, HEADS_PER_CHUNK))


def kernel(x_prompt, x_sample, cache_k, cache_v, state_conv, g_mix_norm, w_in, q_norm, k_norm, sinks,
           w_dw, b_dw, g_conv_ln, b_conv_ln, w_out, g_mlp_norm, w_up, w_down):
    batch, seq, _ = x_prompt.shape
    dec_batch, dec_seq, _ = x_sample.shape
    depth = w_in.shape[0]
    win = cache_k.shape[2]
    assert win == WINDOW == BLOCK and seq % BLOCK == 0 and PAST_LEN >= win

    xp = x_prompt.reshape(batch * seq, D_MODEL)
    xs = x_sample.reshape(dec_batch * dec_seq, D_MODEL)
    tm_p = min(512, batch * seq)
    tm_s = min(512, dec_batch * dec_seq)
    tm_mlp_p = min(MLP_ROWS, batch * seq)
    bb = min(8, dec_batch)
    outs = [[] for _ in range(6)]
    for l in range(depth):
        w_in_l, w_out_l = w_in[l].astype(BF16), w_out[l].astype(BF16)
        w_up_l, w_down_l = w_up[l].astype(BF16), w_down[l].astype(BF16)
        g_mix, g_mlp = g_mix_norm[l].reshape(1, D_MODEL), g_mlp_norm[l].reshape(1, D_MODEL)
        qg, kg = _tile_heads(q_norm[l]), _tile_heads(k_norm[l])
        conv_vecs = (w_dw[l], b_dw[l].reshape(1, CONV_CH), g_conv_ln[l].reshape(1, CONV_CH),
                     b_conv_ln[l].reshape(1, CONV_CH))

        q, k, v, u = _proj(xp, g_mix, w_in_l, qg, kg, tm_p)
        attn_o = _attn_prompt(sinks[l], q, k, v, batch, seq)
        conv_o = _conv_prompt(u, *conv_vecs, batch, seq, min(512, seq))
        xp = _merge_mlp(xp, attn_o, conv_o, w_out_l, g_mlp, w_up_l, w_down_l, tm_mlp_p)
        outs[0].append(k.reshape(batch, seq, N_KV_HEADS, HEAD_DIM)[:, seq - win:])
        outs[1].append(v.reshape(batch, seq, N_KV_HEADS, HEAD_DIM)[:, seq - win:])
        outs[2].append(u.reshape(batch, seq, CONV_CH)[:, seq - CONV_BUF:])

        q, k, v, u = _proj(xs, g_mix, w_in_l, qg, kg, tm_s)
        attn_o, nk, nv = _attn_sample(sinks[l], q, k, v,
                                      cache_k[l].reshape(dec_batch, win, KV_W),
                                      cache_v[l].reshape(dec_batch, win, KV_W), dec_seq, bb)
        conv_o, ns = _conv_sample(u, state_conv[l], *conv_vecs, dec_seq, bb)
        xs = _merge_mlp(xs, attn_o, conv_o, w_out_l, g_mlp, w_up_l, w_down_l, tm_s)
        outs[3].append(nk.reshape(dec_batch, win, N_KV_HEADS, HEAD_DIM))
        outs[4].append(nv.reshape(dec_batch, win, N_KV_HEADS, HEAD_DIM))
        outs[5].append(ns)

    return (xp.reshape(batch, seq, D_MODEL), xs.reshape(dec_batch, dec_seq, D_MODEL),
            *(jnp.stack(o) for o in outs))
```

```python
import functools

import jax
import jax.numpy as jnp
from jax import lax
from jax.experimental import pallas as pl
from jax.experimental.pallas import tpu as pltpu

F32 = jnp.float32
BF16 = jnp.bfloat16

D_MODEL = 2048
HEAD_DIM = 64
ATTN_W = D_MODEL // 2
N_HEADS = ATTN_W // HEAD_DIM
N_KV_HEADS = N_HEADS // 4
KV_W = N_KV_HEADS * HEAD_DIM
CONV_CH = D_MODEL - ATTN_W
IN_W = ATTN_W + 2 * KV_W + 2 * CONV_CH
WINDOW = 128
BLOCK = 128
CONV_W = 31
CONV_BUF = CONV_W - 1
D_FF = 4 * D_MODEL
EPS = 1e-6
ATTN_SCALE = HEAD_DIM ** -0.5
NEG = -1e30
PAST_LEN = 16384

LANES = 128
SUBLANES = 8
HEADS_PER_CHUNK = LANES // HEAD_DIM
Q_CHUNKS = ATTN_W // LANES
KV_CHUNKS = KV_W // LANES
CONV_HALO = 32
VMEM_LIMIT = 56 * 1024 * 1024
MLP_ROWS = 1024
MLP_FF_CHUNK = 512
ATTN_UNROLL = 3
MERGE_ROWS = 256


def _dot(a, b):
    return jnp.dot(a, b, preferred_element_type=F32)


def _dot_nt(a, b):
    return lax.dot_general(a, b, (((1,), (1,)), ((), ())), preferred_element_type=F32)


def _params(n_axes):
    return pltpu.CompilerParams(dimension_semantics=("arbitrary",) * n_axes,
                                vmem_limit_bytes=VMEM_LIMIT)


def _resident(shape):
    return pl.BlockSpec(shape, lambda *_: (0,) * len(shape), pipeline_mode=pl.Buffered(1))


def _rms(x, gain):
    ms = jnp.mean(x * x, axis=-1, keepdims=True)
    return x * lax.rsqrt(ms + EPS) * gain


def _head_rms_chunk(zc, is_lo):
    sq = zc * zc
    sq_lo = jnp.where(is_lo, sq, 0.0)
    sum_lo = jnp.sum(sq_lo, axis=-1, keepdims=True)
    sum_hi = jnp.sum(sq - sq_lo, axis=-1, keepdims=True)
    ms = jnp.where(is_lo, sum_lo, sum_hi) * (1.0 / HEAD_DIM)
    return zc * lax.rsqrt(ms + EPS)


def _proj_kernel(x_ref, g_ref, w_ref, qg_ref, kg_ref, q_ref, k_ref, v_ref, u_ref):
    tm = x_ref.shape[0]
    h = _rms(x_ref[...], g_ref[...]).astype(BF16)
    is_lo = lax.broadcasted_iota(jnp.int32, (tm, LANES), 1) < HEAD_DIM

    zq = _dot(h, w_ref[:, :ATTN_W])
    qg = qg_ref[...] * ATTN_SCALE
    for c in range(Q_CHUNKS):
        sl = slice(c * LANES, (c + 1) * LANES)
        q_ref[:, sl] = (_head_rms_chunk(zq[:, sl], is_lo) * qg).astype(BF16)

    zk = _dot(h, w_ref[:, ATTN_W:ATTN_W + KV_W])
    for c in range(KV_CHUNKS):
        sl = slice(c * LANES, (c + 1) * LANES)
        k_ref[:, sl] = _head_rms_chunk(zk[:, sl], is_lo) * kg_ref[...]

    v_ref[...] = _dot(h, w_ref[:, ATTN_W + KV_W:ATTN_W + 2 * KV_W])

    a0 = ATTN_W + 2 * KV_W
    a = _dot(h, w_ref[:, a0:a0 + CONV_CH])
    gate = _dot(h, w_ref[:, a0 + CONV_CH:])
    u_ref[...] = a * jax.nn.sigmoid(gate)


def _proj(x, g_mix, w_in, qg, kg, tm):
    t = x.shape[0]
    row = lambda w: pl.BlockSpec((tm, w), lambda i: (i, 0))
    return pl.pallas_call(
        _proj_kernel,
        grid=(t // tm,),
        in_specs=[row(D_MODEL), _resident((1, D_MODEL)), _resident((D_MODEL, IN_W)),
                  _resident((1, LANES)), _resident((1, LANES))],
        out_specs=[row(ATTN_W), row(KV_W), row(KV_W), row(CONV_CH)],
        out_shape=[jax.ShapeDtypeStruct((t, ATTN_W), BF16),
                   jax.ShapeDtypeStruct((t, KV_W), F32),
                   jax.ShapeDtypeStruct((t, KV_W), F32),
                   jax.ShapeDtypeStruct((t, CONV_CH), F32)],
        compiler_params=_params(1),
        name="proj",
    )(x, g_mix, w_in, qg, kg)


def _split_heads(chunk, is_lo):
    swapped = pltpu.roll(chunk, HEAD_DIM, axis=1)
    head_a = (jnp.where(is_lo, chunk, 0.0), jnp.where(is_lo, 0.0, swapped))
    head_b = (jnp.where(is_lo, swapped, 0.0), jnp.where(is_lo, 0.0, chunk))
    return head_a, head_b


def _softmax_parts(s_list, sink):
    m = sink
    for s in s_list:
        m = jnp.maximum(m, jnp.max(s, axis=-1, keepdims=True))
    p_list = [jnp.exp(s - m) for s in s_list]
    den = jnp.exp(sink - m)
    for p in p_list:
        den = den + jnp.sum(p, axis=-1, keepdims=True)
    return p_list, 1.0 / den


def _attn_prompt_kernel(sink_ref, q_ref, k_ref, v_ref, o_ref,
                        kk_ref, vx_ref, s_ref, p_ref, es_ref):
    n_blk = q_ref.shape[0] // BLOCK
    is_lo = lax.broadcasted_iota(jnp.int32, (BLOCK, LANES), 1) < HEAD_DIM
    tri = (lax.broadcasted_iota(jnp.int32, (BLOCK, BLOCK), 1)
           <= lax.broadcasted_iota(jnp.int32, (BLOCK, BLOCK), 0))
    tri2 = ((lax.broadcasted_iota(jnp.int32, (2 * BLOCK, 2 * BLOCK), 1) & (BLOCK - 1))
            <= (lax.broadcasted_iota(jnp.int32, (2 * BLOCK, 2 * BLOCK), 0) & (BLOCK - 1)))
    ones_pat = ((lax.broadcasted_iota(jnp.int32, (2 * BLOCK, LANES), 0) < BLOCK)
                == (lax.broadcasted_iota(jnp.int32, (2 * BLOCK, LANES), 1) < HEAD_DIM)).astype(BF16)

    def build(n, carry):
        rows = pl.ds(pl.multiple_of(n * BLOCK, BLOCK), BLOCK)
        for c in range(KV_CHUNKS):
            sl = slice(c * LANES, (c + 1) * LANES)
            k_heads = _split_heads(k_ref[rows, sl], is_lo)
            v_heads = _split_heads(v_ref[rows, sl], is_lo)
            for half in range(HEADS_PER_CHUNK):
                g = HEADS_PER_CHUNK * c + half
                kk_ref[g, n, :BLOCK, :] = k_heads[half][0].astype(BF16)
                kk_ref[g, n, BLOCK:, :] = k_heads[half][1].astype(BF16)
                vx_ref[g, n, :BLOCK, :LANES] = v_heads[half][0].astype(BF16)
                vx_ref[g, n, BLOCK:, :LANES] = v_heads[half][1].astype(BF16)
                vx_ref[g, n, :, LANES:] = ones_pat
        return carry

    def scores(n, has_prev, slot):
        rows = pl.ds(pl.multiple_of(n * BLOCK, BLOCK), BLOCK)
        for g in range(N_KV_HEADS):
            q2 = q_ref[rows, 2 * g * LANES:2 * (g + 1) * LANES]
            qs = jnp.concatenate([q2[:, :LANES], q2[:, LANES:]], axis=0)
            if has_prev:
                s2 = _dot_nt(qs, kk_ref[g, pl.ds(n - 1, 2)].reshape(4 * BLOCK, LANES))
                s_ref[slot, g] = jnp.where(tri2, s2[:, 2 * BLOCK:], s2[:, :2 * BLOCK])
            else:
                s_ref[slot, g] = jnp.where(tri2, _dot_nt(qs, kk_ref[g, n]), NEG)

    def weights(has_prev, slot):
        for g in range(N_KV_HEADS):
            for jj in range(HEADS_PER_CHUNK):
                rs = slice(jj * BLOCK, (jj + 1) * BLOCK)
                sink_terms = []
                for e in range(HEADS_PER_CHUNK):
                    sink = sink_ref[HEADS_PER_CHUNK * (HEADS_PER_CHUNK * g + jj) + e]
                    s = s_ref[slot, g, rs, e * BLOCK:(e + 1) * BLOCK]
                    m = jnp.maximum(jnp.max(s, axis=-1, keepdims=True), sink)
                    p = jnp.exp(s - m)
                    p_own = jnp.where(tri, p, 0.0)
                    p_ref[slot, g, rs, (2 + e) * BLOCK:(3 + e) * BLOCK] = p_own.astype(BF16)
                    if has_prev:
                        p_ref[slot, g, rs, e * BLOCK:(e + 1) * BLOCK] = (p - p_own).astype(BF16)
                    sink_terms.append(jnp.exp(sink - m))
                es_ref[slot, g, rs, :] = jnp.where(is_lo, sink_terms[0], sink_terms[1])

    def values(n, has_prev, slot):
        rows = pl.ds(pl.multiple_of(n * BLOCK, BLOCK), BLOCK)
        for g in range(N_KV_HEADS):
            if has_prev:
                ox = _dot(p_ref[slot, g], vx_ref[g, pl.ds(n - 1, 2)].reshape(4 * BLOCK, 2 * LANES))
            else:
                ox = _dot(p_ref[slot, g, :, 2 * BLOCK:], vx_ref[g, n])
            o = ox[:, :LANES] / (ox[:, LANES:] + es_ref[slot, g])
            for jj in range(HEADS_PER_CHUNK):
                j = HEADS_PER_CHUNK * g + jj
                o_ref[rows, j * LANES:(j + 1) * LANES] = o[jj * BLOCK:(jj + 1) * BLOCK].astype(BF16)

    def blocks(ns, has_prev):
        for n in ns:
            build(n, 0)
        for slot, n in enumerate(ns):
            scores(n, has_prev, slot)
        for slot in range(len(ns)):
            weights(has_prev, slot)
        for slot, n in enumerate(ns):
            values(n, has_prev, slot)

    blocks([0], False)
    n_pairs = (n_blk - 1) // ATTN_UNROLL

    def body(i, carry):
        first = 1 + ATTN_UNROLL * i
        blocks([first + d for d in range(ATTN_UNROLL)], True)
        return carry

    lax.fori_loop(0, n_pairs, body, 0)
    for n in range(1 + ATTN_UNROLL * n_pairs, n_blk):
        blocks([n], True)


def _attn_prompt(sinks, q, k, v, batch, seq):
    n_blk = seq // BLOCK
    row = lambda w: pl.BlockSpec((seq, w), lambda b: (b, 0))
    stage = lambda w, dt: pltpu.VMEM((ATTN_UNROLL, N_KV_HEADS, 2 * BLOCK, w), dt)
    return pl.pallas_call(
        _attn_prompt_kernel,
        grid=(batch,),
        in_specs=[pl.BlockSpec(memory_space=pltpu.SMEM), row(ATTN_W), row(KV_W), row(KV_W)],
        out_specs=row(ATTN_W),
        out_shape=jax.ShapeDtypeStruct((batch * seq, ATTN_W), BF16),
        scratch_shapes=[pltpu.VMEM((N_KV_HEADS, n_blk, 2 * BLOCK, LANES), BF16),
                        pltpu.VMEM((N_KV_HEADS, n_blk, 2 * BLOCK, 2 * LANES), BF16),
                        stage(2 * BLOCK, F32), stage(4 * BLOCK, BF16), stage(LANES, F32)],
        compiler_params=_params(1),
        name="attn_prompt",
    )(sinks, q, k, v)


def _attn_sample_kernel(sink_ref, q_ref, k_ref, v_ref, ck_ref, cv_ref,
                        o_ref, nk_ref, nv_ref, kk_ref, vv_ref, *, dec_seq):
    bb = ck_ref.shape[0]
    rows = bb * dec_seq
    win = ck_ref.shape[1]
    is_lo = lax.broadcasted_iota(jnp.int32, (win, LANES), 1) < HEAD_DIM
    is_lo_r = lax.broadcasted_iota(jnp.int32, (rows, LANES), 1) < HEAD_DIM

    for b in range(bb):
        new = slice(b * dec_seq, (b + 1) * dec_seq)
        nk_ref[b, :win - dec_seq, :] = ck_ref[b, dec_seq:, :]
        nk_ref[b, win - dec_seq:, :] = k_ref[new, :]
        nv_ref[b, :win - dec_seq, :] = cv_ref[b, dec_seq:, :]
        nv_ref[b, win - dec_seq:, :] = v_ref[new, :]

    for b in range(bb):
        for c in range(KV_CHUNKS):
            sl = slice(c * LANES, (c + 1) * LANES)
            for src, dst in ((ck_ref, kk_ref), (cv_ref, vv_ref)):
                for half, (lo, hi) in enumerate(_split_heads(src[b, :, sl], is_lo)):
                    g = HEADS_PER_CHUNK * c + half
                    dst[b, g, :win, :] = lo.astype(BF16)
                    dst[b, g, win:, :] = hi.astype(BF16)

    r_id = lax.broadcasted_iota(jnp.int32, (rows, win), 0)
    c_id = lax.broadcasted_iota(jnp.int32, (rows, win), 1)
    win_mask = c_id > r_id % dec_seq
    rn = lax.broadcasted_iota(jnp.int32, (rows, rows), 0)
    cn = lax.broadcasted_iota(jnp.int32, (rows, rows), 1)
    new_mask = (rn // dec_seq == cn // dec_seq) & (cn % dec_seq <= rn % dec_seq)
    seq_of_row = lax.broadcasted_iota(jnp.int32, (rows, LANES), 0) // dec_seq

    for g in range(N_KV_HEADS):
        c, half = divmod(g, HEADS_PER_CHUNK)
        sl_kv = slice(c * LANES, (c + 1) * LANES)
        kn = _split_heads(k_ref[:, sl_kv], is_lo_r)[half]
        vn = _split_heads(v_ref[:, sl_kv], is_lo_r)[half]
        kk_new = jnp.concatenate([kn[0], kn[1]], axis=0).astype(BF16)
        vv_new = jnp.concatenate([vn[0], vn[1]], axis=0).astype(BF16)
        for jj in range(HEADS_PER_CHUNK):
            j = HEADS_PER_CHUNK * g + jj
            sl = slice(j * LANES, (j + 1) * LANES)
            qj = q_ref[:, sl]
            s_new = _dot_nt(qj, kk_new)
            s_win = jnp.zeros((rows, 2 * win), F32)
            for b in range(bb):
                sb = _dot_nt(qj, kk_ref[b, g])
                s_win = jnp.where(jnp.concatenate([seq_of_row, seq_of_row], axis=1) == b, sb, s_win)
            p_win, p_new, inv = [], [], []
            for e in range(HEADS_PER_CHUNK):
                sw = jnp.where(win_mask, s_win[:, e * win:(e + 1) * win], NEG)
                sn = jnp.where(new_mask, s_new[:, e * rows:(e + 1) * rows], NEG)
                (pw, pn), r = _softmax_parts([sw, sn], sink_ref[HEADS_PER_CHUNK * j + e])
                p_win.append(pw.astype(BF16))
                p_new.append(pn.astype(BF16))
                inv.append(r)
            p_win = jnp.concatenate(p_win, axis=1)
            o = _dot(jnp.concatenate(p_new, axis=1), vv_new)
            for b in range(bb):
                o = o + jnp.where(seq_of_row == b, _dot(p_win, vv_ref[b, g]), 0.0)
            o_ref[:, sl] = (o * jnp.where(is_lo_r, inv[0], inv[1])).astype(BF16)


def _attn_sample(sinks, q, k, v, cache_k, cache_v, dec_seq, bb):
    dec_batch, win, _ = cache_k.shape
    rows = bb * dec_seq
    row = lambda w: pl.BlockSpec((rows, w), lambda i: (i, 0))
    cache = pl.BlockSpec((bb, win, KV_W), lambda i: (i, 0, 0))
    return pl.pallas_call(
        functools.partial(_attn_sample_kernel, dec_seq=dec_seq),
        grid=(dec_batch // bb,),
        in_specs=[pl.BlockSpec(memory_space=pltpu.SMEM), row(ATTN_W), row(KV_W), row(KV_W), cache, cache],
        out_specs=[row(ATTN_W), cache, cache],
        out_shape=[jax.ShapeDtypeStruct((dec_batch * dec_seq, ATTN_W), BF16),
                   jax.ShapeDtypeStruct(cache_k.shape, F32),
                   jax.ShapeDtypeStruct(cache_v.shape, F32)],
        scratch_shapes=[pltpu.VMEM((bb, N_KV_HEADS, 2 * win, LANES), BF16),
                        pltpu.VMEM((bb, N_KV_HEADS, 2 * win, LANES), BF16)],
        compiler_params=_params(1),
        name="attn_sample",
    )(sinks, q, k, v, cache_k, cache_v)


def _ln_swish(y, g, b):
    mu = jnp.mean(y, axis=-1, keepdims=True)
    yc = y - mu
    yn = yc * lax.rsqrt(jnp.mean(yc * yc, axis=-1, keepdims=True) + EPS) * g + b
    return yn * jax.nn.sigmoid(yn)


def _conv_prompt_kernel(u_ref, w_ref, b_ref, g_ref, bl_ref, o_ref, sh_ref, wb_ref, y_ref,
                        *, sub_rows, sub_cols):
    tr = u_ref.shape[0]
    n_col = CONV_CH // sub_cols
    i = pl.program_id(1)

    @pl.when(i == 0)
    def _():
        sh_ref[:, :, :CONV_HALO, :] = jnp.zeros((SUBLANES, n_col, CONV_HALO, sub_cols), F32)

    @pl.when(i > 0)
    def _():
        sh_ref[:, :, :CONV_HALO, :] = sh_ref[:, :, tr:tr + CONV_HALO, :]

    @pl.when((pl.program_id(0) == 0) & (i == 0))
    def _():
        for tap in range(CONV_W):
            for c in range(n_col):
                wb_ref[tap, c] = jnp.broadcast_to(w_ref[tap:tap + 1, c * sub_cols:(c + 1) * sub_cols],
                                                  (SUBLANES, sub_cols))

    for c in range(n_col):
        sh_ref[0, c, CONV_HALO:, :] = u_ref[:, c * sub_cols:(c + 1) * sub_cols]
        x = sh_ref[0, c, CONV_HALO - SUBLANES:, :]
        for r in range(1, SUBLANES):
            sh_ref[r, c, CONV_HALO:, :] = pltpu.roll(x, r, axis=0)[SUBLANES:]

    groups = sub_rows // SUBLANES

    def piece(idx, carry):
        rb, c = idx // n_col, idx % n_col
        r0 = pl.multiple_of(rb * sub_rows, sub_rows)
        acc = [jnp.zeros((SUBLANES, sub_cols), F32)] * groups
        for r in range(SUBLANES):
            n_a = (CONV_W - 1 - r) // SUBLANES + 1
            w_taps = [wb_ref[CONV_W - 1 - (SUBLANES * a + r), c] for a in range(n_a)]
            for m in range(1 - n_a, groups):
                start = pl.multiple_of(r0 + (CONV_HALO + SUBLANES * m), SUBLANES)
                x = sh_ref[r, c, pl.ds(start, SUBLANES), :]
                for a in range(n_a):
                    if 0 <= m + a < groups:
                        acc[m + a] = acc[m + a] + w_taps[a] * x
        y_ref[c, pl.ds(r0, sub_rows), :] = jnp.concatenate(acc, axis=0)
        return carry

    lax.fori_loop(0, (tr // sub_rows) * n_col, piece, 0)

    for r0 in range(0, tr, 32):
        y = jnp.concatenate([y_ref[c, r0:r0 + 32, :] for c in range(n_col)], axis=1)
        y = _ln_swish(y + b_ref[...], g_ref[...], bl_ref[...])
        o_ref[r0:r0 + 32, :] = y.astype(BF16)


def _conv_prompt(u, w_dw, b_dw, g_ln, b_ln, batch, seq, tr):
    n_t = seq // tr
    sub_rows, sub_cols = 128, 128
    n_col = CONV_CH // sub_cols
    tile = pl.BlockSpec((tr, CONV_CH), lambda b, i: (b * n_t + i, 0))
    vec = _resident((1, CONV_CH))
    return pl.pallas_call(
        functools.partial(_conv_prompt_kernel, sub_rows=sub_rows, sub_cols=sub_cols),
        grid=(batch, n_t),
        in_specs=[tile, _resident((CONV_W, CONV_CH)), vec, vec, vec],
        out_specs=tile,
        out_shape=jax.ShapeDtypeStruct((batch * seq, CONV_CH), BF16),
        scratch_shapes=[pltpu.VMEM((SUBLANES, n_col, CONV_HALO + tr, sub_cols), F32),
                        pltpu.VMEM((CONV_W, n_col, SUBLANES, sub_cols), F32),
                        pltpu.VMEM((n_col, tr, sub_cols), F32)],
        compiler_params=_params(2),
        name="conv_prompt",
    )(u, w_dw, b_dw, g_ln, b_ln)


def _conv_sample_kernel(u_ref, st_ref, w_ref, b_ref, g_ref, bl_ref, o_ref, ns_ref, ext_ref, y_ref,
                        *, dec_seq):
    bb = st_ref.shape[0]
    for b in range(bb):
        new = slice(b * dec_seq, (b + 1) * dec_seq)
        ext_ref[:CONV_BUF, :] = st_ref[b]
        ext_ref[CONV_BUF:CONV_BUF + dec_seq, :] = u_ref[new, :]
        ns_ref[b] = ext_ref[dec_seq:dec_seq + CONV_BUF, :]
        acc = jnp.zeros((dec_seq, CONV_CH), F32)
        for j in range(CONV_W):
            acc = acc + w_ref[j:j + 1, :] * ext_ref[j:j + dec_seq, :]
        y_ref[new, :] = acc
    o_ref[...] = _ln_swish(y_ref[...] + b_ref[...], g_ref[...], bl_ref[...]).astype(BF16)


def _conv_sample(u, state, w_dw, b_dw, g_ln, b_ln, dec_seq, bb):
    dec_batch = state.shape[0]
    rows = bb * dec_seq
    tile = pl.BlockSpec((rows, CONV_CH), lambda i: (i, 0))
    st = pl.BlockSpec((bb, CONV_BUF, CONV_CH), lambda i: (i, 0, 0))
    vec = _resident((1, CONV_CH))
    return pl.pallas_call(
        functools.partial(_conv_sample_kernel, dec_seq=dec_seq),
        grid=(dec_batch // bb,),
        in_specs=[tile, st, _resident((CONV_W, CONV_CH)), vec, vec, vec],
        out_specs=[tile, st],
        out_shape=[jax.ShapeDtypeStruct((dec_batch * dec_seq, CONV_CH), BF16),
                   jax.ShapeDtypeStruct(state.shape, F32)],
        scratch_shapes=[pltpu.VMEM((CONV_BUF + 2 * dec_seq + 8, CONV_CH), F32),
                        pltpu.VMEM((rows, CONV_CH), F32)],
        compiler_params=_params(1),
        name="conv_sample",
    )(u, state, w_dw, b_dw, g_ln, b_ln)


def _merge_mlp_kernel(x_hbm, a_hbm, c_hbm, wo_ref, g_ref, wu_ref, wd_ref, o_ref,
                      x_buf, a_buf, c_buf, h_ref, sem, *, merge_rows):
    i, j = pl.program_id(0), pl.program_id(1)
    tm = o_ref.shape[0]

    def row_copies(tile):
        half = tm // 2
        base = pl.multiple_of(tile * tm, tm)
        rows = pl.ds(base, tm)
        return (pltpu.make_async_copy(x_hbm.at[pl.ds(base, half)], x_buf.at[:half], sem.at[0]),
                pltpu.make_async_copy(x_hbm.at[pl.ds(base + half, half)], x_buf.at[half:], sem.at[1]),
                pltpu.make_async_copy(a_hbm.at[rows], a_buf, sem.at[2]),
                pltpu.make_async_copy(c_hbm.at[rows], c_buf, sem.at[3]))

    @pl.when((i == 0) & (j == 0))
    def _():
        for cp in row_copies(0):
            cp.start()

    @pl.when(j == 0)
    def _():
        for cp in row_copies(i):
            cp.wait()
        for r0 in range(0, tm, merge_rows):
            rs = slice(r0, r0 + merge_rows)
            mix = jnp.concatenate([a_buf[rs, :], c_buf[rs, :]], axis=1)
            x1 = x_buf[rs, :] + _dot(mix, wo_ref[...])
            o_ref[rs, :] = x1
            h_ref[rs, :] = _rms(x1, g_ref[...]).astype(BF16)

    for step, cp in enumerate(row_copies(i + 1), start=1):
        @pl.when((j == step) & (i + 1 < pl.num_programs(0)))
        def _(cp=cp):
            cp.start()

    up = _dot(h_ref[...], wu_ref[...])
    act = jnp.square(jnp.maximum(up, 0.0)).astype(BF16)
    o_ref[...] += _dot(act, wd_ref[...])


def _merge_mlp(x, attn_o, conv_o, w_out, g_mlp, w_up, w_down, tm):
    t = x.shape[0]
    n_ff = D_FF // MLP_FF_CHUNK
    merge_rows = min(MERGE_ROWS, tm)
    assert t % tm == 0 and tm % merge_rows == 0 and n_ff >= 5
    hbm = pl.BlockSpec(memory_space=pl.ANY)
    return pl.pallas_call(
        functools.partial(_merge_mlp_kernel, merge_rows=merge_rows),
        grid=(t // tm, n_ff),
        in_specs=[hbm, hbm, hbm, _resident((D_MODEL, D_MODEL)), _resident((1, D_MODEL)),
                  pl.BlockSpec((D_MODEL, MLP_FF_CHUNK), lambda i, j: (0, j)),
                  pl.BlockSpec((MLP_FF_CHUNK, D_MODEL), lambda i, j: (j, 0))],
        out_specs=pl.BlockSpec((tm, D_MODEL), lambda i, j: (i, 0)),
        out_shape=jax.ShapeDtypeStruct((t, D_MODEL), F32),
        scratch_shapes=[pltpu.VMEM((tm, D_MODEL), F32), pltpu.VMEM((tm, ATTN_W), BF16),
                        pltpu.VMEM((tm, CONV_CH), BF16), pltpu.VMEM((tm, D_MODEL), BF16),
                        pltpu.SemaphoreType.DMA((4,))],
        compiler_params=_params(2),
        name="merge_mlp",
    )(x, attn_o, conv_o, w_out, g_mlp, w_up, w_down)


def _tile_heads(gain):
    return jnp.tile(gain.reshape(1, HEAD_DIM), (1, HEADS_PER_CHUNK))


def kernel(x_prompt, x_sample, cache_k, cache_v, state_conv, g_mix_norm, w_in, q_norm, k_norm, sinks,
           w_dw, b_dw, g_conv_ln, b_conv_ln, w_out, g_mlp_norm, w_up, w_down):
    batch, seq, _ = x_prompt.shape
    dec_batch, dec_seq, _ = x_sample.shape
    depth = w_in.shape[0]
    win = cache_k.shape[2]
    assert win == WINDOW == BLOCK and seq % BLOCK == 0 and PAST_LEN >= win

    xp = x_prompt.reshape(batch * seq, D_MODEL)
    xs = x_sample.reshape(dec_batch * dec_seq, D_MODEL)
    tm_p = min(512, batch * seq)
    tm_s = min(512, dec_batch * dec_seq)
    tm_mlp_p = min(MLP_ROWS, batch * seq)
    bb = min(8, dec_batch)
    outs = [[] for _ in range(6)]
    for l in range(depth):
        w_in_l, w_out_l = w_in[l].astype(BF16), w_out[l].astype(BF16)
        w_up_l, w_down_l = w_up[l].astype(BF16), w_down[l].astype(BF16)
        g_mix, g_mlp = g_mix_norm[l].reshape(1, D_MODEL), g_mlp_norm[l].reshape(1, D_MODEL)
        qg, kg = _tile_heads(q_norm[l]), _tile_heads(k_norm[l])
        conv_vecs = (w_dw[l], b_dw[l].reshape(1, CONV_CH), g_conv_ln[l].reshape(1, CONV_CH),
                     b_conv_ln[l].reshape(1, CONV_CH))

        q, k, v, u = _proj(xp, g_mix, w_in_l, qg, kg, tm_p)
        attn_o = _attn_prompt(sinks[l], q, k, v, batch, seq)
        conv_o = _conv_prompt(u, *conv_vecs, batch, seq, min(512, seq))
        xp = _merge_mlp(xp, attn_o, conv_o, w_out_l, g_mlp, w_up_l, w_down_l, tm_mlp_p)
        outs[0].append(k.reshape(batch, seq, N_KV_HEADS, HEAD_DIM)[:, seq - win:])
        outs[1].append(v.reshape(batch, seq, N_KV_HEADS, HEAD_DIM)[:, seq - win:])
        outs[2].append(u.reshape(batch, seq, CONV_CH)[:, seq - CONV_BUF:])

        q, k, v, u = _proj(xs, g_mix, w_in_l, qg, kg, tm_s)
        attn_o, nk, nv = _attn_sample(sinks[l], q, k, v,
                                      cache_k[l].reshape(dec_batch, win, KV_W),
                                      cache_v[l].reshape(dec_batch, win, KV_W), dec_seq, bb)
        conv_o, ns = _conv_sample(u, state_conv[l], *conv_vecs, dec_seq, bb)
        xs = _merge_mlp(xs, attn_o, conv_o, w_out_l, g_mlp, w_up_l, w_down_l, tm_s)
        outs[3].append(nk.reshape(dec_batch, win, N_KV_HEADS, HEAD_DIM))
        outs[4].append(nv.reshape(dec_batch, win, N_KV_HEADS, HEAD_DIM))
        outs[5].append(ns)

    return (xp.reshape(batch, seq, D_MODEL), xs.reshape(dec_batch, dec_seq, D_MODEL),
            *(jnp.stack(o) for o in outs))
```

```python
import functools

import jax
import jax.numpy as jnp
from jax import lax
from jax.experimental import pallas as pl
from jax.experimental.pallas import tpu as pltpu

F32 = jnp.float32
BF16 = jnp.bfloat16

D_MODEL = 2048
HEAD_DIM = 64
ATTN_W = D_MODEL // 2
N_HEADS = ATTN_W // HEAD_DIM
N_KV_HEADS = N_HEADS // 4
KV_W = N_KV_HEADS * HEAD_DIM
CONV_CH = D_MODEL - ATTN_W
IN_W = ATTN_W + 2 * KV_W + 2 * CONV_CH
WINDOW = 128
BLOCK = 128
CONV_W = 31
CONV_BUF = CONV_W - 1
D_FF = 4 * D_MODEL
EPS = 1e-6
ATTN_SCALE = HEAD_DIM ** -0.5
NEG = -1e30
PAST_LEN = 16384

LANES = 128
SUBLANES = 8
HEADS_PER_CHUNK = LANES // HEAD_DIM
Q_CHUNKS = ATTN_W // LANES
KV_CHUNKS = KV_W // LANES
CONV_HALO = 32
VMEM_LIMIT = 56 * 1024 * 1024
MLP_ROWS = 1024
MLP_FF_CHUNK = 512
ATTN_UNROLL = 3
MERGE_ROWS = 256


def _dot(a, b):
    return jnp.dot(a, b, preferred_element_type=F32)


def _dot_nt(a, b):
    return lax.dot_general(a, b, (((1,), (1,)), ((), ())), preferred_element_type=F32)


def _params(n_axes):
    return pltpu.CompilerParams(dimension_semantics=("arbitrary",) * n_axes,
                                vmem_limit_bytes=VMEM_LIMIT)


def _resident(shape):
    return pl.BlockSpec(shape, lambda *_: (0,) * len(shape), pipeline_mode=pl.Buffered(1))


def _rms(x, gain):
    ms = jnp.mean(x * x, axis=-1, keepdims=True)
    return x * lax.rsqrt(ms + EPS) * gain


def _head_rms_chunk(zc, is_lo):
    sq = zc * zc
    sq_lo = jnp.where(is_lo, sq, 0.0)
    sum_lo = jnp.sum(sq_lo, axis=-1, keepdims=True)
    sum_hi = jnp.sum(sq - sq_lo, axis=-1, keepdims=True)
    ms = jnp.where(is_lo, sum_lo, sum_hi) * (1.0 / HEAD_DIM)
    return zc * lax.rsqrt(ms + EPS)


def _proj_kernel(x_ref, g_ref, w_ref, qg_ref, kg_ref, q_ref, k_ref, v_ref, u_ref):
    tm = x_ref.shape[0]
    h = _rms(x_ref[...], g_ref[...]).astype(BF16)
    is_lo = lax.broadcasted_iota(jnp.int32, (tm, LANES), 1) < HEAD_DIM

    zq = _dot(h, w_ref[:, :ATTN_W])
    qg = qg_ref[...] * ATTN_SCALE
    for c in range(Q_CHUNKS):
        sl = slice(c * LANES, (c + 1) * LANES)
        q_ref[:, sl] = (_head_rms_chunk(zq[:, sl], is_lo) * qg).astype(BF16)

    zk = _dot(h, w_ref[:, ATTN_W:ATTN_W + KV_W])
    for c in range(KV_CHUNKS):
        sl = slice(c * LANES, (c + 1) * LANES)
        k_ref[:, sl] = _head_rms_chunk(zk[:, sl], is_lo) * kg_ref[...]

    v_ref[...] = _dot(h, w_ref[:, ATTN_W + KV_W:ATTN_W + 2 * KV_W])

    a0 = ATTN_W + 2 * KV_W
    a = _dot(h, w_ref[:, a0:a0 + CONV_CH])
    gate = _dot(h, w_ref[:, a0 + CONV_CH:])
    u_ref[...] = a * jax.nn.sigmoid(gate)


def _proj(x, g_mix, w_in, qg, kg, tm):
    t = x.shape[0]
    row = lambda w: pl.BlockSpec((tm, w), lambda i: (i, 0))
    return pl.pallas_call(
        _proj_kernel,
        grid=(t // tm,),
        in_specs=[row(D_MODEL), _resident((1, D_MODEL)), _resident((D_MODEL, IN_W)),
                  _resident((1, LANES)), _resident((1, LANES))],
        out_specs=[row(ATTN_W), row(KV_W), row(KV_W), row(CONV_CH)],
        out_shape=[jax.ShapeDtypeStruct((t, ATTN_W), BF16),
                   jax.ShapeDtypeStruct((t, KV_W), F32),
                   jax.ShapeDtypeStruct((t, KV_W), F32),
                   jax.ShapeDtypeStruct((t, CONV_CH), F32)],
        compiler_params=_params(1),
        name="proj",
    )(x, g_mix, w_in, qg, kg)


def _split_heads(chunk, is_lo):
    swapped = pltpu.roll(chunk, HEAD_DIM, axis=1)
    head_a = (jnp.where(is_lo, chunk, 0.0), jnp.where(is_lo, 0.0, swapped))
    head_b = (jnp.where(is_lo, swapped, 0.0), jnp.where(is_lo, 0.0, chunk))
    return head_a, head_b


def _softmax_parts(s_list, sink):
    m = sink
    for s in s_list:
        m = jnp.maximum(m, jnp.max(s, axis=-1, keepdims=True))
    p_list = [jnp.exp(s - m) for s in s_list]
    den = jnp.exp(sink - m)
    for p in p_list:
        den = den + jnp.sum(p, axis=-1, keepdims=True)
    return p_list, 1.0 / den


def _attn_prompt_kernel(sink_ref, q_ref, k_ref, v_ref, o_ref,
                        kk_ref, vx_ref, s_ref, p_ref, es_ref):
    n_blk = q_ref.shape[0] // BLOCK
    is_lo = lax.broadcasted_iota(jnp.int32, (BLOCK, LANES), 1) < HEAD_DIM
    tri = (lax.broadcasted_iota(jnp.int32, (BLOCK, BLOCK), 1)
           <= lax.broadcasted_iota(jnp.int32, (BLOCK, BLOCK), 0))
    tri2 = ((lax.broadcasted_iota(jnp.int32, (2 * BLOCK, 2 * BLOCK), 1) & (BLOCK - 1))
            <= (lax.broadcasted_iota(jnp.int32, (2 * BLOCK, 2 * BLOCK), 0) & (BLOCK - 1)))
    ones_pat = ((lax.broadcasted_iota(jnp.int32, (2 * BLOCK, LANES), 0) < BLOCK)
                == (lax.broadcasted_iota(jnp.int32, (2 * BLOCK, LANES), 1) < HEAD_DIM)).astype(BF16)

    def build(n, carry):
        rows = pl.ds(pl.multiple_of(n * BLOCK, BLOCK), BLOCK)
        for c in range(KV_CHUNKS):
            sl = slice(c * LANES, (c + 1) * LANES)
            k_heads = _split_heads(k_ref[rows, sl], is_lo)
            v_heads = _split_heads(v_ref[rows, sl], is_lo)
            for half in range(HEADS_PER_CHUNK):
                g = HEADS_PER_CHUNK * c + half
                kk_ref[g, n, :BLOCK, :] = k_heads[half][0].astype(BF16)
                kk_ref[g, n, BLOCK:, :] = k_heads[half][1].astype(BF16)
                vx_ref[g, n, :BLOCK, :LANES] = v_heads[half][0].astype(BF16)
                vx_ref[g, n, BLOCK:, :LANES] = v_heads[half][1].astype(BF16)
                vx_ref[g, n, :, LANES:] = ones_pat
        return carry

    def scores(n, has_prev, slot):
        rows = pl.ds(pl.multiple_of(n * BLOCK, BLOCK), BLOCK)
        for g in range(N_KV_HEADS):
            q2 = q_ref[rows, 2 * g * LANES:2 * (g + 1) * LANES]
            qs = jnp.concatenate([q2[:, :LANES], q2[:, LANES:]], axis=0)
            if has_prev:
                s2 = _dot_nt(qs, kk_ref[g, pl.ds(n - 1, 2)].reshape(4 * BLOCK, LANES))
                s_ref[slot, g] = jnp.where(tri2, s2[:, 2 * BLOCK:], s2[:, :2 * BLOCK])
            else:
                s_ref[slot, g] = jnp.where(tri2, _dot_nt(qs, kk_ref[g, n]), NEG)

    def weights(has_prev, slot):
        for g in range(N_KV_HEADS):
            for jj in range(HEADS_PER_CHUNK):
                rs = slice(jj * BLOCK, (jj + 1) * BLOCK)
                sink_terms = []
                for e in range(HEADS_PER_CHUNK):
                    sink = sink_ref[HEADS_PER_CHUNK * (HEADS_PER_CHUNK * g + jj) + e]
                    s = s_ref[slot, g, rs, e * BLOCK:(e + 1) * BLOCK]
                    m = jnp.maximum(jnp.max(s, axis=-1, keepdims=True), sink)
                    p = jnp.exp(s - m)
                    p_own = jnp.where(tri, p, 0.0)
                    p_ref[slot, g, rs, (2 + e) * BLOCK:(3 + e) * BLOCK] = p_own.astype(BF16)
                    if has_prev:
                        p_ref[slot, g, rs, e * BLOCK:(e + 1) * BLOCK] = (p - p_own).astype(BF16)
                    sink_terms.append(jnp.exp(sink - m))
                es_ref[slot, g, rs, :] = jnp.where(is_lo, sink_terms[0], sink_terms[1])

    def values(n, has_prev, slot):
        rows = pl.ds(pl.multiple_of(n * BLOCK, BLOCK), BLOCK)
        for g in range(N_KV_HEADS):
            if has_prev:
                ox = _dot(p_ref[slot, g], vx_ref[g, pl.ds(n - 1, 2)].reshape(4 * BLOCK, 2 * LANES))
            else:
                ox = _dot(p_ref[slot, g, :, 2 * BLOCK:], vx_ref[g, n])
            o = ox[:, :LANES] / (ox[:, LANES:] + es_ref[slot, g])
            for jj in range(HEADS_PER_CHUNK):
                j = HEADS_PER_CHUNK * g + jj
                o_ref[rows, j * LANES:(j + 1) * LANES] = o[jj * BLOCK:(jj + 1) * BLOCK].astype(BF16)

    def blocks(ns, has_prev):
        for n in ns:
            build(n, 0)
        for slot, n in enumerate(ns):
            scores(n, has_prev, slot)
        for slot in range(len(ns)):
            weights(has_prev, slot)
        for slot, n in enumerate(ns):
            values(n, has_prev, slot)

    blocks([0], False)
    n_pairs = (n_blk - 1) // ATTN_UNROLL

    def body(i, carry):
        first = 1 + ATTN_UNROLL * i
        blocks([first + d for d in range(ATTN_UNROLL)], True)
        return carry

    lax.fori_loop(0, n_pairs, body, 0)
    for n in range(1 + ATTN_UNROLL * n_pairs, n_blk):
        blocks([n], True)


def _attn_prompt(sinks, q, k, v, batch, seq):
    n_blk = seq // BLOCK
    row = lambda w: pl.BlockSpec((seq, w), lambda b: (b, 0))
    stage = lambda w, dt: pltpu.VMEM((ATTN_UNROLL, N_KV_HEADS, 2 * BLOCK, w), dt)
    return pl.pallas_call(
        _attn_prompt_kernel,
        grid=(batch,),
        in_specs=[pl.BlockSpec(memory_space=pltpu.SMEM), row(ATTN_W), row(KV_W), row(KV_W)],
        out_specs=row(ATTN_W),
        out_shape=jax.ShapeDtypeStruct((batch * seq, ATTN_W), BF16),
        scratch_shapes=[pltpu.VMEM((N_KV_HEADS, n_blk, 2 * BLOCK, LANES), BF16),
                        pltpu.VMEM((N_KV_HEADS, n_blk, 2 * BLOCK, 2 * LANES), BF16),
                        stage(2 * BLOCK, F32), stage(4 * BLOCK, BF16), stage(LANES, F32)],
        compiler_params=_params(1),
        name="attn_prompt",
    )(sinks, q, k, v)


def _attn_sample_kernel(sink_ref, q_ref, k_ref, v_ref, ck_ref, cv_ref,
                        o_ref, nk_ref, nv_ref, kk_ref, vv_ref, sw_ref, sn_ref, pw_ref, pn_ref, inv_ref,
                        *, dec_seq):
    bb = ck_ref.shape[0]
    rows = bb * dec_seq
    win = ck_ref.shape[1]
    is_lo = lax.broadcasted_iota(jnp.int32, (win, LANES), 1) < HEAD_DIM
    is_lo_r = lax.broadcasted_iota(jnp.int32, (rows, LANES), 1) < HEAD_DIM

    for b in range(bb):
        new = slice(b * dec_seq, (b + 1) * dec_seq)
        nk_ref[b, :win - dec_seq, :] = ck_ref[b, dec_seq:, :]
        nk_ref[b, win - dec_seq:, :] = k_ref[new, :]
        nv_ref[b, :win - dec_seq, :] = cv_ref[b, dec_seq:, :]
        nv_ref[b, win - dec_seq:, :] = v_ref[new, :]

    for b in range(bb):
        for c in range(KV_CHUNKS):
            sl = slice(c * LANES, (c + 1) * LANES)
            for src, dst in ((ck_ref, kk_ref), (cv_ref, vv_ref)):
                for half, (lo, hi) in enumerate(_split_heads(src[b, :, sl], is_lo)):
                    g = HEADS_PER_CHUNK * c + half
                    dst[b, g, :win, :] = lo.astype(BF16)
                    dst[b, g, win:, :] = hi.astype(BF16)

    r_id = lax.broadcasted_iota(jnp.int32, (rows, win), 0)
    c_id = lax.broadcasted_iota(jnp.int32, (rows, win), 1)
    win_mask = c_id > r_id % dec_seq
    rn = lax.broadcasted_iota(jnp.int32, (rows, rows), 0)
    cn = lax.broadcasted_iota(jnp.int32, (rows, rows), 1)
    new_mask = (rn // dec_seq == cn // dec_seq) & (cn % dec_seq <= rn % dec_seq)
    seq_of_row = lax.broadcasted_iota(jnp.int32, (rows, LANES), 0) // dec_seq

    seq_of_row2 = jnp.concatenate([seq_of_row, seq_of_row], axis=1)

    def new_rows(src_ref, g):
        c, half = divmod(g, HEADS_PER_CHUNK)
        lo, hi = _split_heads(src_ref[:, c * LANES:(c + 1) * LANES], is_lo_r)[half]
        return jnp.concatenate([lo, hi], axis=0).astype(BF16)

    for g in range(N_KV_HEADS):
        kk_new = new_rows(k_ref, g)
        for jj in range(HEADS_PER_CHUNK):
            j = HEADS_PER_CHUNK * g + jj
            qj = q_ref[:, j * LANES:(j + 1) * LANES]
            sn_ref[j] = _dot_nt(qj, kk_new)
            s_win = jnp.zeros((rows, 2 * win), F32)
            for b in range(bb):
                s_win = jnp.where(seq_of_row2 == b, _dot_nt(qj, kk_ref[b, g]), s_win)
            sw_ref[j] = s_win

    for j in range(Q_CHUNKS):
        inv = []
        for e in range(HEADS_PER_CHUNK):
            sw = jnp.where(win_mask, sw_ref[j, :, e * win:(e + 1) * win], NEG)
            sn = jnp.where(new_mask, sn_ref[j, :, e * rows:(e + 1) * rows], NEG)
            (pw, pn), r = _softmax_parts([sw, sn], sink_ref[HEADS_PER_CHUNK * j + e])
            pw_ref[j, :, e * win:(e + 1) * win] = pw.astype(BF16)
            pn_ref[j, :, e * rows:(e + 1) * rows] = pn.astype(BF16)
            inv.append(r)
        inv_ref[j] = jnp.where(is_lo_r, inv[0], inv[1])

    for g in range(N_KV_HEADS):
        vv_new = new_rows(v_ref, g)
        for jj in range(HEADS_PER_CHUNK):
            j = HEADS_PER_CHUNK * g + jj
            p_win = pw_ref[j]
            o = _dot(pn_ref[j], vv_new)
            for b in range(bb):
                o = o + jnp.where(seq_of_row == b, _dot(p_win, vv_ref[b, g]), 0.0)
            o_ref[:, j * LANES:(j + 1) * LANES] = (o * inv_ref[j]).astype(BF16)


def _attn_sample(sinks, q, k, v, cache_k, cache_v, dec_seq, bb):
    dec_batch, win, _ = cache_k.shape
    rows = bb * dec_seq
    row = lambda w: pl.BlockSpec((rows, w), lambda i: (i, 0))
    cache = pl.BlockSpec((bb, win, KV_W), lambda i: (i, 0, 0))
    return pl.pallas_call(
        functools.partial(_attn_sample_kernel, dec_seq=dec_seq),
        grid=(dec_batch // bb,),
        in_specs=[pl.BlockSpec(memory_space=pltpu.SMEM), row(ATTN_W), row(KV_W), row(KV_W), cache, cache],
        out_specs=[row(ATTN_W), cache, cache],
        out_shape=[jax.ShapeDtypeStruct((dec_batch * dec_seq, ATTN_W), BF16),
                   jax.ShapeDtypeStruct(cache_k.shape, F32),
                   jax.ShapeDtypeStruct(cache_v.shape, F32)],
        scratch_shapes=[pltpu.VMEM((bb, N_KV_HEADS, 2 * win, LANES), BF16),
                        pltpu.VMEM((bb, N_KV_HEADS, 2 * win, LANES), BF16),
                        pltpu.VMEM((Q_CHUNKS, rows, 2 * win), F32), pltpu.VMEM((Q_CHUNKS, rows, 2 * rows), F32),
                        pltpu.VMEM((Q_CHUNKS, rows, 2 * win), BF16), pltpu.VMEM((Q_CHUNKS, rows, 2 * rows), BF16),
                        pltpu.VMEM((Q_CHUNKS, rows, LANES), F32)],
        compiler_params=_params(1),
        name="attn_sample",
    )(sinks, q, k, v, cache_k, cache_v)


def _ln_swish(y, g, b):
    mu = jnp.mean(y, axis=-1, keepdims=True)
    yc = y - mu
    yn = yc * lax.rsqrt(jnp.mean(yc * yc, axis=-1, keepdims=True) + EPS) * g + b
    return yn * jax.nn.sigmoid(yn)


def _conv_prompt_kernel(u_ref, w_ref, b_ref, g_ref, bl_ref, o_ref, sh_ref, wb_ref, y_ref,
                        *, sub_rows, sub_cols):
    tr = u_ref.shape[0]
    n_col = CONV_CH // sub_cols
    i = pl.program_id(1)

    @pl.when(i == 0)
    def _():
        sh_ref[:, :, :CONV_HALO, :] = jnp.zeros((SUBLANES, n_col, CONV_HALO, sub_cols), F32)

    @pl.when(i > 0)
    def _():
        sh_ref[:, :, :CONV_HALO, :] = sh_ref[:, :, tr:tr + CONV_HALO, :]

    @pl.when((pl.program_id(0) == 0) & (i == 0))
    def _():
        for tap in range(CONV_W):
            for c in range(n_col):
                wb_ref[tap, c] = jnp.broadcast_to(w_ref[tap:tap + 1, c * sub_cols:(c + 1) * sub_cols],
                                                  (SUBLANES, sub_cols))

    for c in range(n_col):
        sh_ref[0, c, CONV_HALO:, :] = u_ref[:, c * sub_cols:(c + 1) * sub_cols]
        x = sh_ref[0, c, CONV_HALO - SUBLANES:, :]
        for r in range(1, SUBLANES):
            sh_ref[r, c, CONV_HALO:, :] = pltpu.roll(x, r, axis=0)[SUBLANES:]

    groups = sub_rows // SUBLANES

    def piece(idx, carry):
        rb, c = idx // n_col, idx % n_col
        r0 = pl.multiple_of(rb * sub_rows, sub_rows)
        acc = [jnp.zeros((SUBLANES, sub_cols), F32)] * groups
        for r in range(SUBLANES):
            n_a = (CONV_W - 1 - r) // SUBLANES + 1
            w_taps = [wb_ref[CONV_W - 1 - (SUBLANES * a + r), c] for a in range(n_a)]
            for m in range(1 - n_a, groups):
                start = pl.multiple_of(r0 + (CONV_HALO + SUBLANES * m), SUBLANES)
                x = sh_ref[r, c, pl.ds(start, SUBLANES), :]
                for a in range(n_a):
                    if 0 <= m + a < groups:
                        acc[m + a] = acc[m + a] + w_taps[a] * x
        y_ref[c, pl.ds(r0, sub_rows), :] = jnp.concatenate(acc, axis=0)
        return carry

    lax.fori_loop(0, (tr // sub_rows) * n_col, piece, 0)

    for r0 in range(0, tr, 32):
        y = jnp.concatenate([y_ref[c, r0:r0 + 32, :] for c in range(n_col)], axis=1)
        y = _ln_swish(y + b_ref[...], g_ref[...], bl_ref[...])
        o_ref[r0:r0 + 32, :] = y.astype(BF16)


def _conv_prompt(u, w_dw, b_dw, g_ln, b_ln, batch, seq, tr):
    n_t = seq // tr
    sub_rows, sub_cols = 128, 128
    n_col = CONV_CH // sub_cols
    tile = pl.BlockSpec((tr, CONV_CH), lambda b, i: (b * n_t + i, 0))
    vec = _resident((1, CONV_CH))
    return pl.pallas_call(
        functools.partial(_conv_prompt_kernel, sub_rows=sub_rows, sub_cols=sub_cols),
        grid=(batch, n_t),
        in_specs=[tile, _resident((CONV_W, CONV_CH)), vec, vec, vec],
        out_specs=tile,
        out_shape=jax.ShapeDtypeStruct((batch * seq, CONV_CH), BF16),
        scratch_shapes=[pltpu.VMEM((SUBLANES, n_col, CONV_HALO + tr, sub_cols), F32),
                        pltpu.VMEM((CONV_W, n_col, SUBLANES, sub_cols), F32),
                        pltpu.VMEM((n_col, tr, sub_cols), F32)],
        compiler_params=_params(2),
        name="conv_prompt",
    )(u, w_dw, b_dw, g_ln, b_ln)


def _conv_sample_kernel(u_ref, st_ref, w_ref, b_ref, g_ref, bl_ref, o_ref, ns_ref, ext_ref, wt_ref, y_ref,
                        *, dec_seq):
    bb = st_ref.shape[0]
    ext_rows = ext_ref.shape[0]

    @pl.when(pl.program_id(0) == 0)
    def _():
        ext_ref[...] = jnp.zeros(ext_ref.shape, F32)
        wt_ref[...] = jnp.zeros(wt_ref.shape, F32)
        for t in range(dec_seq):
            wt_ref[t, t:t + CONV_W, :] = w_ref[...]

    for b in range(bb):
        new = slice(b * dec_seq, (b + 1) * dec_seq)
        ext_ref[:CONV_BUF, :] = st_ref[b]
        ext_ref[CONV_BUF:CONV_BUF + dec_seq, :] = u_ref[new, :]
        ns_ref[b] = ext_ref[dec_seq:dec_seq + CONV_BUF, :]
        acc = [jnp.zeros((SUBLANES, CONV_CH), F32)] * dec_seq
        for s0 in range(0, ext_rows, SUBLANES):
            x = ext_ref[s0:s0 + SUBLANES, :]
            for t in range(dec_seq):
                acc[t] = acc[t] + wt_ref[t, s0:s0 + SUBLANES, :] * x
        y_ref[new, :] = jnp.concatenate([jnp.sum(a, axis=0, keepdims=True) for a in acc], axis=0)
    o_ref[...] = _ln_swish(y_ref[...] + b_ref[...], g_ref[...], bl_ref[...]).astype(BF16)


def _conv_sample(u, state, w_dw, b_dw, g_ln, b_ln, dec_seq, bb):
    dec_batch = state.shape[0]
    rows = bb * dec_seq
    ext_rows = -(-(CONV_BUF + dec_seq) // SUBLANES) * SUBLANES
    tile = pl.BlockSpec((rows, CONV_CH), lambda i: (i, 0))
    st = pl.BlockSpec((bb, CONV_BUF, CONV_CH), lambda i: (i, 0, 0))
    vec = _resident((1, CONV_CH))
    return pl.pallas_call(
        functools.partial(_conv_sample_kernel, dec_seq=dec_seq),
        grid=(dec_batch // bb,),
        in_specs=[tile, st, _resident((CONV_W, CONV_CH)), vec, vec, vec],
        out_specs=[tile, st],
        out_shape=[jax.ShapeDtypeStruct((dec_batch * dec_seq, CONV_CH), BF16),
                   jax.ShapeDtypeStruct(state.shape, F32)],
        scratch_shapes=[pltpu.VMEM((ext_rows, CONV_CH), F32),
                        pltpu.VMEM((dec_seq, ext_rows, CONV_CH), F32),
                        pltpu.VMEM((rows, CONV_CH), F32)],
        compiler_params=_params(1),
        name="conv_sample",
    )(u, state, w_dw, b_dw, g_ln, b_ln)


def _merge_mlp_kernel(x_hbm, a_hbm, c_hbm, wo_ref, g_ref, wu_ref, wd_ref, o_ref,
                      x_buf, a_buf, c_buf, h_ref, sem, *, merge_rows):
    i, j = pl.program_id(0), pl.program_id(1)
    tm = o_ref.shape[0]

    def row_copies(tile):
        half = tm // 2
        base = pl.multiple_of(tile * tm, tm)
        rows = pl.ds(base, tm)
        return (pltpu.make_async_copy(x_hbm.at[pl.ds(base, half)], x_buf.at[:half], sem.at[0]),
                pltpu.make_async_copy(x_hbm.at[pl.ds(base + half, half)], x_buf.at[half:], sem.at[1]),
                pltpu.make_async_copy(a_hbm.at[rows], a_buf, sem.at[2]),
                pltpu.make_async_copy(c_hbm.at[rows], c_buf, sem.at[3]))

    @pl.when((i == 0) & (j == 0))
    def _():
        for cp in row_copies(0):
            cp.start()

    @pl.when(j == 0)
    def _():
        for cp in row_copies(i):
            cp.wait()
        for r0 in range(0, tm, merge_rows):
            rs = slice(r0, r0 + merge_rows)
            mix = jnp.concatenate([a_buf[rs, :], c_buf[rs, :]], axis=1)
            x1 = x_buf[rs, :] + _dot(mix, wo_ref[...])
            o_ref[rs, :] = x1
            h_ref[rs, :] = _rms(x1, g_ref[...]).astype(BF16)

    for step, cp in enumerate(row_copies(i + 1), start=1):
        @pl.when((j == step) & (i + 1 < pl.num_programs(0)))
        def _(cp=cp):
            cp.start()

    up = _dot(h_ref[...], wu_ref[...])
    act = jnp.square(jnp.maximum(up, 0.0)).astype(BF16)
    o_ref[...] += _dot(act, wd_ref[...])


def _merge_mlp(x, attn_o, conv_o, w_out, g_mlp, w_up, w_down, tm):
    t = x.shape[0]
    n_ff = D_FF // MLP_FF_CHUNK
    merge_rows = min(MERGE_ROWS, tm)
    assert t % tm == 0 and tm % merge_rows == 0 and n_ff >= 5
    hbm = pl.BlockSpec(memory_space=pl.ANY)
    return pl.pallas_call(
        functools.partial(_merge_mlp_kernel, merge_rows=merge_rows),
        grid=(t // tm, n_ff),
        in_specs=[hbm, hbm, hbm, _resident((D_MODEL, D_MODEL)), _resident((1, D_MODEL)),
                  pl.BlockSpec((D_MODEL, MLP_FF_CHUNK), lambda i, j: (0, j)),
                  pl.BlockSpec((MLP_FF_CHUNK, D_MODEL), lambda i, j: (j, 0))],
        out_specs=pl.BlockSpec((tm, D_MODEL), lambda i, j: (i, 0)),
        out_shape=jax.ShapeDtypeStruct((t, D_MODEL), F32),
        scratch_shapes=[pltpu.VMEM((tm, D_MODEL), F32), pltpu.VMEM((tm, ATTN_W), BF16),
                        pltpu.VMEM((tm, CONV_CH), BF16), pltpu.VMEM((tm, D_MODEL), BF16),
                        pltpu.SemaphoreType.DMA((4,))],
        compiler_params=_params(2),
        name="merge_mlp",
    )(x, attn_o, conv_o, w_out, g_mlp, w_up, w_down)


def _tile_heads(gain):
    return jnp.tile(gain.reshape(1, HEAD_DIM), (1, HEADS_PER_CHUNK))


def kernel(x_prompt, x_sample, cache_k, cache_v, state_conv, g_mix_norm, w_in, q_norm, k_norm, sinks,
           w_dw, b_dw, g_conv_ln, b_conv_ln, w_out, g_mlp_norm, w_up, w_down):
    batch, seq, _ = x_prompt.shape
    dec_batch, dec_seq, _ = x_sample.shape
    depth = w_in.shape[0]
    win = cache_k.shape[2]
    assert win == WINDOW == BLOCK and seq % BLOCK == 0 and PAST_LEN >= win

    xp = x_prompt.reshape(batch * seq, D_MODEL)
    xs = x_sample.reshape(dec_batch * dec_seq, D_MODEL)
    tm_p = min(512, batch * seq)
    tm_s = min(512, dec_batch * dec_seq)
    tm_mlp_p = min(MLP_ROWS, batch * seq)
    bb = min(8, dec_batch)
    outs = [[] for _ in range(6)]
    for l in range(depth):
        w_in_l, w_out_l = w_in[l].astype(BF16), w_out[l].astype(BF16)
        w_up_l, w_down_l = w_up[l].astype(BF16), w_down[l].astype(BF16)
        g_mix, g_mlp = g_mix_norm[l].reshape(1, D_MODEL), g_mlp_norm[l].reshape(1, D_MODEL)
        qg, kg = _tile_heads(q_norm[l]), _tile_heads(k_norm[l])
        conv_vecs = (w_dw[l], b_dw[l].reshape(1, CONV_CH), g_conv_ln[l].reshape(1, CONV_CH),
                     b_conv_ln[l].reshape(1, CONV_CH))

        q, k, v, u = _proj(xp, g_mix, w_in_l, qg, kg, tm_p)
        attn_o = _attn_prompt(sinks[l], q, k, v, batch, seq)
        conv_o = _conv_prompt(u, *conv_vecs, batch, seq, min(512, seq))
        xp = _merge_mlp(xp, attn_o, conv_o, w_out_l, g_mlp, w_up_l, w_down_l, tm_mlp_p)
        outs[0].append(k.reshape(batch, seq, N_KV_HEADS, HEAD_DIM)[:, seq - win:])
        outs[1].append(v.reshape(batch, seq, N_KV_HEADS, HEAD_DIM)[:, seq - win:])
        outs[2].append(u.reshape(batch, seq, CONV_CH)[:, seq - CONV_BUF:])

        q, k, v, u = _proj(xs, g_mix, w_in_l, qg, kg, tm_s)
        attn_o, nk, nv = _attn_sample(sinks[l], q, k, v,
                                      cache_k[l].reshape(dec_batch, win, KV_W),
                                      cache_v[l].reshape(dec_batch, win, KV_W), dec_seq, bb)
        conv_o, ns = _conv_sample(u, state_conv[l], *conv_vecs, dec_seq, bb)
        xs = _merge_mlp(xs, attn_o, conv_o, w_out_l, g_mlp, w_up_l, w_down_l, tm_s)
        outs[3].append(nk.reshape(dec_batch, win, N_KV_HEADS, HEAD_DIM))
        outs[4].append(nv.reshape(dec_batch, win, N_KV_HEADS, HEAD_DIM))
        outs[5].append(ns)

    return (xp.reshape(batch, seq, D_MODEL), xs.reshape(dec_batch, dec_seq, D_MODEL),
            *(jnp.stack(o) for o in outs))
```

```python
import functools

import jax
import jax.numpy as jnp
from jax import lax
from jax.experimental import pallas as pl
from jax.experimental.pallas import tpu as pltpu

F32 = jnp.float32
BF16 = jnp.bfloat16

D_MODEL = 2048
HEAD_DIM = 64
ATTN_W = D_MODEL // 2
N_HEADS = ATTN_W // HEAD_DIM
N_KV_HEADS = N_HEADS // 4
KV_W = N_KV_HEADS * HEAD_DIM
CONV_CH = D_MODEL - ATTN_W
IN_W = ATTN_W + 2 * KV_W + 2 * CONV_CH
WINDOW = 128
BLOCK = 128
CONV_W = 31
CONV_BUF = CONV_W - 1
D_FF = 4 * D_MODEL
EPS = 1e-6
ATTN_SCALE = HEAD_DIM ** -0.5
NEG = -1e30
PAST_LEN = 16384

LANES = 128
SUBLANES = 8
HEADS_PER_CHUNK = LANES // HEAD_DIM
Q_CHUNKS = ATTN_W // LANES
KV_CHUNKS = KV_W // LANES
CONV_HALO = 32
VMEM_LIMIT = 56 * 1024 * 1024
MLP_ROWS = 1024
MLP_FF_CHUNK = 512
ATTN_UNROLL = 3
MERGE_ROWS = 256


def _dot(a, b):
    return jnp.dot(a, b, preferred_element_type=F32)


def _dot_nt(a, b):
    return lax.dot_general(a, b, (((1,), (1,)), ((), ())), preferred_element_type=F32)


def _params(n_axes):
    return pltpu.CompilerParams(dimension_semantics=("arbitrary",) * n_axes,
                                vmem_limit_bytes=VMEM_LIMIT)


def _resident(shape):
    return pl.BlockSpec(shape, lambda *_: (0,) * len(shape), pipeline_mode=pl.Buffered(1))


def _rms(x, gain):
    ms = jnp.mean(x * x, axis=-1, keepdims=True)
    return x * lax.rsqrt(ms + EPS) * gain


def _head_rms_chunk(zc, is_lo):
    sq = zc * zc
    sq_lo = jnp.where(is_lo, sq, 0.0)
    sum_lo = jnp.sum(sq_lo, axis=-1, keepdims=True)
    sum_hi = jnp.sum(sq - sq_lo, axis=-1, keepdims=True)
    ms = jnp.where(is_lo, sum_lo, sum_hi) * (1.0 / HEAD_DIM)
    return zc * lax.rsqrt(ms + EPS)


def _proj_kernel(x_ref, g_ref, w_ref, qg_ref, kg_ref, q_ref, k_ref, v_ref, u_ref):
    tm = x_ref.shape[0]
    h = _rms(x_ref[...], g_ref[...]).astype(BF16)
    is_lo = lax.broadcasted_iota(jnp.int32, (tm, LANES), 1) < HEAD_DIM

    zq = _dot(h, w_ref[:, :ATTN_W])
    qg = qg_ref[...] * ATTN_SCALE
    for c in range(Q_CHUNKS):
        sl = slice(c * LANES, (c + 1) * LANES)
        q_ref[:, sl] = (_head_rms_chunk(zq[:, sl], is_lo) * qg).astype(BF16)

    zk = _dot(h, w_ref[:, ATTN_W:ATTN_W + KV_W])
    for c in range(KV_CHUNKS):
        sl = slice(c * LANES, (c + 1) * LANES)
        k_ref[:, sl] = _head_rms_chunk(zk[:, sl], is_lo) * kg_ref[...]

    v_ref[...] = _dot(h, w_ref[:, ATTN_W + KV_W:ATTN_W + 2 * KV_W])

    a0 = ATTN_W + 2 * KV_W
    a = _dot(h, w_ref[:, a0:a0 + CONV_CH])
    gate = _dot(h, w_ref[:, a0 + CONV_CH:])
    u_ref[...] = a * jax.nn.sigmoid(gate)


def _proj(x, g_mix, w_in, qg, kg, tm):
    t = x.shape[0]
    row = lambda w: pl.BlockSpec((tm, w), lambda i: (i, 0))
    return pl.pallas_call(
        _proj_kernel,
        grid=(t // tm,),
        in_specs=[row(D_MODEL), _resident((1, D_MODEL)), _resident((D_MODEL, IN_W)),
                  _resident((1, LANES)), _resident((1, LANES))],
        out_specs=[row(ATTN_W), row(KV_W), row(KV_W), row(CONV_CH)],
        out_shape=[jax.ShapeDtypeStruct((t, ATTN_W), BF16),
                   jax.ShapeDtypeStruct((t, KV_W), F32),
                   jax.ShapeDtypeStruct((t, KV_W), F32),
                   jax.ShapeDtypeStruct((t, CONV_CH), F32)],
        compiler_params=_params(1),
        name="proj",
    )(x, g_mix, w_in, qg, kg)


def _split_heads(chunk, is_lo):
    swapped = pltpu.roll(chunk, HEAD_DIM, axis=1)
    head_a = (jnp.where(is_lo, chunk, 0.0), jnp.where(is_lo, 0.0, swapped))
    head_b = (jnp.where(is_lo, swapped, 0.0), jnp.where(is_lo, 0.0, chunk))
    return head_a, head_b


def _softmax_parts(s_list, sink):
    m = sink
    for s in s_list:
        m = jnp.maximum(m, jnp.max(s, axis=-1, keepdims=True))
    p_list = [jnp.exp(s - m) for s in s_list]
    den = jnp.exp(sink - m)
    for p in p_list:
        den = den + jnp.sum(p, axis=-1, keepdims=True)
    return p_list, 1.0 / den


def _attn_prompt_kernel(sink_ref, q_ref, k_ref, v_ref, o_ref,
                        kk_ref, vx_ref, s_ref, p_ref, es_ref):
    n_blk = q_ref.shape[0] // BLOCK
    is_lo = lax.broadcasted_iota(jnp.int32, (BLOCK, LANES), 1) < HEAD_DIM
    tri = (lax.broadcasted_iota(jnp.int32, (BLOCK, BLOCK), 1)
           <= lax.broadcasted_iota(jnp.int32, (BLOCK, BLOCK), 0))
    tri2 = ((lax.broadcasted_iota(jnp.int32, (2 * BLOCK, 2 * BLOCK), 1) & (BLOCK - 1))
            <= (lax.broadcasted_iota(jnp.int32, (2 * BLOCK, 2 * BLOCK), 0) & (BLOCK - 1)))
    ones_pat = ((lax.broadcasted_iota(jnp.int32, (2 * BLOCK, LANES), 0) < BLOCK)
                == (lax.broadcasted_iota(jnp.int32, (2 * BLOCK, LANES), 1) < HEAD_DIM)).astype(BF16)

    def build(n, carry):
        rows = pl.ds(pl.multiple_of(n * BLOCK, BLOCK), BLOCK)
        for c in range(KV_CHUNKS):
            sl = slice(c * LANES, (c + 1) * LANES)
            k_heads = _split_heads(k_ref[rows, sl], is_lo)
            v_heads = _split_heads(v_ref[rows, sl], is_lo)
            for half in range(HEADS_PER_CHUNK):
                g = HEADS_PER_CHUNK * c + half
                kk_ref[g, n, :BLOCK, :] = k_heads[half][0].astype(BF16)
                kk_ref[g, n, BLOCK:, :] = k_heads[half][1].astype(BF16)
                vx_ref[g, n, :BLOCK, :LANES] = v_heads[half][0].astype(BF16)
                vx_ref[g, n, BLOCK:, :LANES] = v_heads[half][1].astype(BF16)
                vx_ref[g, n, :, LANES:] = ones_pat
        return carry

    def scores(n, has_prev, slot):
        rows = pl.ds(pl.multiple_of(n * BLOCK, BLOCK), BLOCK)
        for g in range(N_KV_HEADS):
            q2 = q_ref[rows, 2 * g * LANES:2 * (g + 1) * LANES]
            qs = jnp.concatenate([q2[:, :LANES], q2[:, LANES:]], axis=0)
            if has_prev:
                s2 = _dot_nt(qs, kk_ref[g, pl.ds(n - 1, 2)].reshape(4 * BLOCK, LANES))
                s_ref[slot, g] = jnp.where(tri2, s2[:, 2 * BLOCK:], s2[:, :2 * BLOCK])
            else:
                s_ref[slot, g] = jnp.where(tri2, _dot_nt(qs, kk_ref[g, n]), NEG)

    def weights(has_prev, slot):
        for g in range(N_KV_HEADS):
            for jj in range(HEADS_PER_CHUNK):
                rs = slice(jj * BLOCK, (jj + 1) * BLOCK)
                sink_terms = []
                for e in range(HEADS_PER_CHUNK):
                    sink = sink_ref[HEADS_PER_CHUNK * (HEADS_PER_CHUNK * g + jj) + e]
                    s = s_ref[slot, g, rs, e * BLOCK:(e + 1) * BLOCK]
                    m = jnp.maximum(jnp.max(s, axis=-1, keepdims=True), sink)
                    p = jnp.exp(s - m)
                    p_own = jnp.where(tri, p, 0.0)
                    p_ref[slot, g, rs, (2 + e) * BLOCK:(3 + e) * BLOCK] = p_own.astype(BF16)
                    if has_prev:
                        p_ref[slot, g, rs, e * BLOCK:(e + 1) * BLOCK] = (p - p_own).astype(BF16)
                    sink_terms.append(jnp.exp(sink - m))
                es_ref[slot, g, rs, :] = jnp.where(is_lo, sink_terms[0], sink_terms[1])

    def values(n, has_prev, slot):
        rows = pl.ds(pl.multiple_of(n * BLOCK, BLOCK), BLOCK)
        for g in range(N_KV_HEADS):
            if has_prev:
                ox = _dot(p_ref[slot, g], vx_ref[g, pl.ds(n - 1, 2)].reshape(4 * BLOCK, 2 * LANES))
            else:
                ox = _dot(p_ref[slot, g, :, 2 * BLOCK:], vx_ref[g, n])
            o = ox[:, :LANES] / (ox[:, LANES:] + es_ref[slot, g])
            for jj in range(HEADS_PER_CHUNK):
                j = HEADS_PER_CHUNK * g + jj
                o_ref[rows, j * LANES:(j + 1) * LANES] = o[jj * BLOCK:(jj + 1) * BLOCK].astype(BF16)

    def blocks(ns, has_prev):
        for n in ns:
            build(n, 0)
        for slot, n in enumerate(ns):
            scores(n, has_prev, slot)
        for slot in range(len(ns)):
            weights(has_prev, slot)
        for slot, n in enumerate(ns):
            values(n, has_prev, slot)

    blocks([0], False)
    n_pairs = (n_blk - 1) // ATTN_UNROLL

    def body(i, carry):
        first = 1 + ATTN_UNROLL * i
        blocks([first + d for d in range(ATTN_UNROLL)], True)
        return carry

    lax.fori_loop(0, n_pairs, body, 0)
    for n in range(1 + ATTN_UNROLL * n_pairs, n_blk):
        blocks([n], True)


def _attn_prompt(sinks, q, k, v, batch, seq):
    n_blk = seq // BLOCK
    row = lambda w: pl.BlockSpec((seq, w), lambda b: (b, 0))
    stage = lambda w, dt: pltpu.VMEM((ATTN_UNROLL, N_KV_HEADS, 2 * BLOCK, w), dt)
    return pl.pallas_call(
        _attn_prompt_kernel,
        grid=(batch,),
        in_specs=[pl.BlockSpec(memory_space=pltpu.SMEM), row(ATTN_W), row(KV_W), row(KV_W)],
        out_specs=row(ATTN_W),
        out_shape=jax.ShapeDtypeStruct((batch * seq, ATTN_W), BF16),
        scratch_shapes=[pltpu.VMEM((N_KV_HEADS, n_blk, 2 * BLOCK, LANES), BF16),
                        pltpu.VMEM((N_KV_HEADS, n_blk, 2 * BLOCK, 2 * LANES), BF16),
                        stage(2 * BLOCK, F32), stage(4 * BLOCK, BF16), stage(LANES, F32)],
        compiler_params=_params(1),
        name="attn_prompt",
    )(sinks, q, k, v)


def _attn_sample_kernel(sink_ref, q_ref, k_ref, v_ref, ck_ref, cv_ref,
                        o_ref, nk_ref, nv_ref, kk_ref, vv_ref, sw_ref, sn_ref, pw_ref, pn_ref, inv_ref,
                        *, dec_seq):
    bb = ck_ref.shape[0]
    rows = bb * dec_seq
    win = ck_ref.shape[1]
    is_lo = lax.broadcasted_iota(jnp.int32, (win, LANES), 1) < HEAD_DIM
    is_lo_r = lax.broadcasted_iota(jnp.int32, (rows, LANES), 1) < HEAD_DIM

    for b in range(bb):
        new = slice(b * dec_seq, (b + 1) * dec_seq)
        nk_ref[b, :win - dec_seq, :] = ck_ref[b, dec_seq:, :]
        nk_ref[b, win - dec_seq:, :] = k_ref[new, :]
        nv_ref[b, :win - dec_seq, :] = cv_ref[b, dec_seq:, :]
        nv_ref[b, win - dec_seq:, :] = v_ref[new, :]

    for b in range(bb):
        for c in range(KV_CHUNKS):
            sl = slice(c * LANES, (c + 1) * LANES)
            for src, dst in ((ck_ref, kk_ref), (cv_ref, vv_ref)):
                for half, (lo, hi) in enumerate(_split_heads(src[b, :, sl], is_lo)):
                    g = HEADS_PER_CHUNK * c + half
                    dst[b, g, :win, :] = lo.astype(BF16)
                    dst[b, g, win:, :] = hi.astype(BF16)

    r_id = lax.broadcasted_iota(jnp.int32, (rows, win), 0)
    c_id = lax.broadcasted_iota(jnp.int32, (rows, win), 1)
    win_mask = c_id > r_id % dec_seq
    rn = lax.broadcasted_iota(jnp.int32, (rows, rows), 0)
    cn = lax.broadcasted_iota(jnp.int32, (rows, rows), 1)
    new_mask = (rn // dec_seq == cn // dec_seq) & (cn % dec_seq <= rn % dec_seq)
    seq_of_row = lax.broadcasted_iota(jnp.int32, (rows, LANES), 0) // dec_seq

    seq_of_row2 = jnp.concatenate([seq_of_row, seq_of_row], axis=1)

    def new_rows(src_ref, g):
        c, half = divmod(g, HEADS_PER_CHUNK)
        lo, hi = _split_heads(src_ref[:, c * LANES:(c + 1) * LANES], is_lo_r)[half]
        return jnp.concatenate([lo, hi], axis=0).astype(BF16)

    for g in range(N_KV_HEADS):
        kk_new = new_rows(k_ref, g)
        for jj in range(HEADS_PER_CHUNK):
            j = HEADS_PER_CHUNK * g + jj
            qj = q_ref[:, j * LANES:(j + 1) * LANES]
            sn_ref[j] = _dot_nt(qj, kk_new)
            s_win = jnp.zeros((rows, 2 * win), F32)
            for b in range(bb):
                s_win = jnp.where(seq_of_row2 == b, _dot_nt(qj, kk_ref[b, g]), s_win)
            sw_ref[j] = s_win

    for j in range(Q_CHUNKS):
        inv = []
        for e in range(HEADS_PER_CHUNK):
            sw = jnp.where(win_mask, sw_ref[j, :, e * win:(e + 1) * win], NEG)
            sn = jnp.where(new_mask, sn_ref[j, :, e * rows:(e + 1) * rows], NEG)
            (pw, pn), r = _softmax_parts([sw, sn], sink_ref[HEADS_PER_CHUNK * j + e])
            pw_ref[j, :, e * win:(e + 1) * win] = pw.astype(BF16)
            pn_ref[j, :, e * rows:(e + 1) * rows] = pn.astype(BF16)
            inv.append(r)
        inv_ref[j] = jnp.where(is_lo_r, inv[0], inv[1])

    for g in range(N_KV_HEADS):
        vv_new = new_rows(v_ref, g)
        for jj in range(HEADS_PER_CHUNK):
            j = HEADS_PER_CHUNK * g + jj
            p_win = pw_ref[j]
            o = _dot(pn_ref[j], vv_new)
            for b in range(bb):
                o = o + jnp.where(seq_of_row == b, _dot(p_win, vv_ref[b, g]), 0.0)
            o_ref[:, j * LANES:(j + 1) * LANES] = (o * inv_ref[j]).astype(BF16)


def _attn_sample(sinks, q, k, v, cache_k, cache_v, dec_seq, bb):
    dec_batch, win, _ = cache_k.shape
    rows = bb * dec_seq
    row = lambda w: pl.BlockSpec((rows, w), lambda i: (i, 0))
    cache = pl.BlockSpec((bb, win, KV_W), lambda i: (i, 0, 0))
    return pl.pallas_call(
        functools.partial(_attn_sample_kernel, dec_seq=dec_seq),
        grid=(dec_batch // bb,),
        in_specs=[pl.BlockSpec(memory_space=pltpu.SMEM), row(ATTN_W), row(KV_W), row(KV_W), cache, cache],
        out_specs=[row(ATTN_W), cache, cache],
        out_shape=[jax.ShapeDtypeStruct((dec_batch * dec_seq, ATTN_W), BF16),
                   jax.ShapeDtypeStruct(cache_k.shape, F32),
                   jax.ShapeDtypeStruct(cache_v.shape, F32)],
        scratch_shapes=[pltpu.VMEM((bb, N_KV_HEADS, 2 * win, LANES), BF16),
                        pltpu.VMEM((bb, N_KV_HEADS, 2 * win, LANES), BF16),
                        pltpu.VMEM((Q_CHUNKS, rows, 2 * win), F32), pltpu.VMEM((Q_CHUNKS, rows, 2 * rows), F32),
                        pltpu.VMEM((Q_CHUNKS, rows, 2 * win), BF16), pltpu.VMEM((Q_CHUNKS, rows, 2 * rows), BF16),
                        pltpu.VMEM((Q_CHUNKS, rows, LANES), F32)],
        compiler_params=_params(1),
        name="attn_sample",
    )(sinks, q, k, v, cache_k, cache_v)


def _ln_swish(y, g, b):
    mu = jnp.mean(y, axis=-1, keepdims=True)
    yc = y - mu
    yn = yc * lax.rsqrt(jnp.mean(yc * yc, axis=-1, keepdims=True) + EPS) * g + b
    return yn * jax.nn.sigmoid(yn)


def _conv_prompt_kernel(u_ref, w_ref, b_ref, g_ref, bl_ref, o_ref, sh_ref, wb_ref, y_ref,
                        *, sub_rows, sub_cols):
    tr = u_ref.shape[0]
    n_col = CONV_CH // sub_cols
    i = pl.program_id(1)

    @pl.when(i == 0)
    def _():
        sh_ref[:, :, :CONV_HALO, :] = jnp.zeros((SUBLANES, n_col, CONV_HALO, sub_cols), F32)

    @pl.when(i > 0)
    def _():
        sh_ref[:, :, :CONV_HALO, :] = sh_ref[:, :, tr:tr + CONV_HALO, :]

    @pl.when((pl.program_id(0) == 0) & (i == 0))
    def _():
        for tap in range(CONV_W):
            for c in range(n_col):
                wb_ref[tap, c] = jnp.broadcast_to(w_ref[tap:tap + 1, c * sub_cols:(c + 1) * sub_cols],
                                                  (SUBLANES, sub_cols))

    for c in range(n_col):
        sh_ref[0, c, CONV_HALO:, :] = u_ref[:, c * sub_cols:(c + 1) * sub_cols]
        x = sh_ref[0, c, CONV_HALO - SUBLANES:, :]
        for r in range(1, SUBLANES):
            sh_ref[r, c, CONV_HALO:, :] = pltpu.roll(x, r, axis=0)[SUBLANES:]

    groups = sub_rows // SUBLANES

    def piece(idx, carry):
        rb, c = idx // n_col, idx % n_col
        r0 = pl.multiple_of(rb * sub_rows, sub_rows)
        acc = [jnp.zeros((SUBLANES, sub_cols), F32)] * groups
        for r in range(SUBLANES):
            n_a = (CONV_W - 1 - r) // SUBLANES + 1
            w_taps = [wb_ref[CONV_W - 1 - (SUBLANES * a + r), c] for a in range(n_a)]
            for m in range(1 - n_a, groups):
                start = pl.multiple_of(r0 + (CONV_HALO + SUBLANES * m), SUBLANES)
                x = sh_ref[r, c, pl.ds(start, SUBLANES), :]
                for a in range(n_a):
                    if 0 <= m + a < groups:
                        acc[m + a] = acc[m + a] + w_taps[a] * x
        y_ref[c, pl.ds(r0, sub_rows), :] = jnp.concatenate(acc, axis=0)
        return carry

    lax.fori_loop(0, (tr // sub_rows) * n_col, piece, 0, unroll=4)

    for r0 in range(0, tr, 32):
        y = jnp.concatenate([y_ref[c, r0:r0 + 32, :] for c in range(n_col)], axis=1)
        y = _ln_swish(y + b_ref[...], g_ref[...], bl_ref[...])
        o_ref[r0:r0 + 32, :] = y.astype(BF16)


def _conv_prompt(u, w_dw, b_dw, g_ln, b_ln, batch, seq, tr):
    n_t = seq // tr
    sub_rows, sub_cols = 128, 128
    n_col = CONV_CH // sub_cols
    tile = pl.BlockSpec((tr, CONV_CH), lambda b, i: (b * n_t + i, 0))
    vec = _resident((1, CONV_CH))
    return pl.pallas_call(
        functools.partial(_conv_prompt_kernel, sub_rows=sub_rows, sub_cols=sub_cols),
        grid=(batch, n_t),
        in_specs=[tile, _resident((CONV_W, CONV_CH)), vec, vec, vec],
        out_specs=tile,
        out_shape=jax.ShapeDtypeStruct((batch * seq, CONV_CH), BF16),
        scratch_shapes=[pltpu.VMEM((SUBLANES, n_col, CONV_HALO + tr, sub_cols), F32),
                        pltpu.VMEM((CONV_W, n_col, SUBLANES, sub_cols), F32),
                        pltpu.VMEM((n_col, tr, sub_cols), F32)],
        compiler_params=_params(2),
        name="conv_prompt",
    )(u, w_dw, b_dw, g_ln, b_ln)


def _conv_sample_kernel(u_ref, st_ref, w_ref, b_ref, g_ref, bl_ref, o_ref, ns_ref, ext_ref, wt_ref, y_ref,
                        *, dec_seq):
    bb = st_ref.shape[0]
    ext_rows = ext_ref.shape[0]

    @pl.when(pl.program_id(0) == 0)
    def _():
        ext_ref[...] = jnp.zeros(ext_ref.shape, F32)
        wt_ref[...] = jnp.zeros(wt_ref.shape, F32)
        for t in range(dec_seq):
            wt_ref[t, t:t + CONV_W, :] = w_ref[...]

    for b in range(bb):
        new = slice(b * dec_seq, (b + 1) * dec_seq)
        ext_ref[:CONV_BUF, :] = st_ref[b]
        ext_ref[CONV_BUF:CONV_BUF + dec_seq, :] = u_ref[new, :]
        ns_ref[b] = ext_ref[dec_seq:dec_seq + CONV_BUF, :]
        acc = [jnp.zeros((SUBLANES, CONV_CH), F32)] * dec_seq
        for s0 in range(0, ext_rows, SUBLANES):
            x = ext_ref[s0:s0 + SUBLANES, :]
            for t in range(dec_seq):
                acc[t] = acc[t] + wt_ref[t, s0:s0 + SUBLANES, :] * x
        y_ref[new, :] = jnp.concatenate([jnp.sum(a, axis=0, keepdims=True) for a in acc], axis=0)
    o_ref[...] = _ln_swish(y_ref[...] + b_ref[...], g_ref[...], bl_ref[...]).astype(BF16)


def _conv_sample(u, state, w_dw, b_dw, g_ln, b_ln, dec_seq, bb):
    dec_batch = state.shape[0]
    rows = bb * dec_seq
    ext_rows = -(-(CONV_BUF + dec_seq) // SUBLANES) * SUBLANES
    tile = pl.BlockSpec((rows, CONV_CH), lambda i: (i, 0))
    st = pl.BlockSpec((bb, CONV_BUF, CONV_CH), lambda i: (i, 0, 0))
    vec = _resident((1, CONV_CH))
    return pl.pallas_call(
        functools.partial(_conv_sample_kernel, dec_seq=dec_seq),
        grid=(dec_batch // bb,),
        in_specs=[tile, st, _resident((CONV_W, CONV_CH)), vec, vec, vec],
        out_specs=[tile, st],
        out_shape=[jax.ShapeDtypeStruct((dec_batch * dec_seq, CONV_CH), BF16),
                   jax.ShapeDtypeStruct(state.shape, F32)],
        scratch_shapes=[pltpu.VMEM((ext_rows, CONV_CH), F32),
                        pltpu.VMEM((dec_seq, ext_rows, CONV_CH), F32),
                        pltpu.VMEM((rows, CONV_CH), F32)],
        compiler_params=_params(1),
        name="conv_sample",
    )(u, state, w_dw, b_dw, g_ln, b_ln)


def _merge_mlp_kernel(x_hbm, a_hbm, c_hbm, wo_ref, g_ref, wu_ref, wd_ref, o_ref,
                      x_buf, a_buf, c_buf, h_ref, sem, *, merge_rows):
    i, j = pl.program_id(0), pl.program_id(1)
    tm = o_ref.shape[0]

    def row_copies(tile):
        half = tm // 2
        base = pl.multiple_of(tile * tm, tm)
        rows = pl.ds(base, tm)
        return (pltpu.make_async_copy(x_hbm.at[pl.ds(base, half)], x_buf.at[:half], sem.at[0]),
                pltpu.make_async_copy(x_hbm.at[pl.ds(base + half, half)], x_buf.at[half:], sem.at[1]),
                pltpu.make_async_copy(a_hbm.at[rows], a_buf, sem.at[2]),
                pltpu.make_async_copy(c_hbm.at[rows], c_buf, sem.at[3]))

    @pl.when((i == 0) & (j == 0))
    def _():
        for cp in row_copies(0):
            cp.start()

    @pl.when(j == 0)
    def _():
        for cp in row_copies(i):
            cp.wait()
        for r0 in range(0, tm, merge_rows):
            rs = slice(r0, r0 + merge_rows)
            mix = jnp.concatenate([a_buf[rs, :], c_buf[rs, :]], axis=1)
            x1 = x_buf[rs, :] + _dot(mix, wo_ref[...])
            o_ref[rs, :] = x1
            h_ref[rs, :] = _rms(x1, g_ref[...]).astype(BF16)

    for step, cp in enumerate(row_copies(i + 1), start=1):
        @pl.when((j == step) & (i + 1 < pl.num_programs(0)))
        def _(cp=cp):
            cp.start()

    up = _dot(h_ref[...], wu_ref[...])
    act = jnp.square(jnp.maximum(up, 0.0)).astype(BF16)
    o_ref[...] += _dot(act, wd_ref[...])


def _merge_mlp(x, attn_o, conv_o, w_out, g_mlp, w_up, w_down, tm):
    t = x.shape[0]
    n_ff = D_FF // MLP_FF_CHUNK
    merge_rows = min(MERGE_ROWS, tm)
    assert t % tm == 0 and tm % merge_rows == 0 and n_ff >= 5
    hbm = pl.BlockSpec(memory_space=pl.ANY)
    return pl.pallas_call(
        functools.partial(_merge_mlp_kernel, merge_rows=merge_rows),
        grid=(t // tm, n_ff),
        in_specs=[hbm, hbm, hbm, _resident((D_MODEL, D_MODEL)), _resident((1, D_MODEL)),
                  pl.BlockSpec((D_MODEL, MLP_FF_CHUNK), lambda i, j: (0, j)),
                  pl.BlockSpec((MLP_FF_CHUNK, D_MODEL), lambda i, j: (j, 0))],
        out_specs=pl.BlockSpec((tm, D_MODEL), lambda i, j: (i, 0)),
        out_shape=jax.ShapeDtypeStruct((t, D_MODEL), F32),
        scratch_shapes=[pltpu.VMEM((tm, D_MODEL), F32), pltpu.VMEM((tm, ATTN_W), BF16),
                        pltpu.VMEM((tm, CONV_CH), BF16), pltpu.VMEM((tm, D_MODEL), BF16),
                        pltpu.SemaphoreType.DMA((4,))],
        compiler_params=_params(2),
        name="merge_mlp",
    )(x, attn_o, conv_o, w_out, g_mlp, w_up, w_down)


def _tile_heads(gain):
    return jnp.tile(gain.reshape(1, HEAD_DIM), (1, HEADS_PER_CHUNK))


def kernel(x_prompt, x_sample, cache_k, cache_v, state_conv, g_mix_norm, w_in, q_norm, k_norm, sinks,
           w_dw, b_dw, g_conv_ln, b_conv_ln, w_out, g_mlp_norm, w_up, w_down):
    batch, seq, _ = x_prompt.shape
    dec_batch, dec_seq, _ = x_sample.shape
    depth = w_in.shape[0]
    win = cache_k.shape[2]
    assert win == WINDOW == BLOCK and seq % BLOCK == 0 and PAST_LEN >= win

    xp = x_prompt.reshape(batch * seq, D_MODEL)
    xs = x_sample.reshape(dec_batch * dec_seq, D_MODEL)
    tm_p = min(512, batch * seq)
    tm_s = min(512, dec_batch * dec_seq)
    tm_mlp_p = min(MLP_ROWS, batch * seq)
    bb = min(8, dec_batch)
    outs = [[] for _ in range(6)]
    for l in range(depth):
        w_in_l, w_out_l = w_in[l].astype(BF16), w_out[l].astype(BF16)
        w_up_l, w_down_l = w_up[l].astype(BF16), w_down[l].astype(BF16)
        g_mix, g_mlp = g_mix_norm[l].reshape(1, D_MODEL), g_mlp_norm[l].reshape(1, D_MODEL)
        qg, kg = _tile_heads(q_norm[l]), _tile_heads(k_norm[l])
        conv_vecs = (w_dw[l], b_dw[l].reshape(1, CONV_CH), g_conv_ln[l].reshape(1, CONV_CH),
                     b_conv_ln[l].reshape(1, CONV_CH))

        q, k, v, u = _proj(xp, g_mix, w_in_l, qg, kg, tm_p)
        attn_o = _attn_prompt(sinks[l], q, k, v, batch, seq)
        conv_o = _conv_prompt(u, *conv_vecs, batch, seq, min(512, seq))
        xp = _merge_mlp(xp, attn_o, conv_o, w_out_l, g_mlp, w_up_l, w_down_l, tm_mlp_p)
        outs[0].append(k.reshape(batch, seq, N_KV_HEADS, HEAD_DIM)[:, seq - win:])
        outs[1].append(v.reshape(batch, seq, N_KV_HEADS, HEAD_DIM)[:, seq - win:])
        outs[2].append(u.reshape(batch, seq, CONV_CH)[:, seq - CONV_BUF:])

        q, k, v, u = _proj(xs, g_mix, w_in_l, qg, kg, tm_s)
        attn_o, nk, nv = _attn_sample(sinks[l], q, k, v,
                                      cache_k[l].reshape(dec_batch, win, KV_W),
                                      cache_v[l].reshape(dec_batch, win, KV_W), dec_seq, bb)
        conv_o, ns = _conv_sample(u, state_conv[l], *conv_vecs, dec_seq, bb)
        xs = _merge_mlp(xs, attn_o, conv_o, w_out_l, g_mlp, w_up_l, w_down_l, tm_s)
        outs[3].append(nk.reshape(dec_batch, win, N_KV_HEADS, HEAD_DIM))
        outs[4].append(nv.reshape(dec_batch, win, N_KV_HEADS, HEAD_DIM))
        outs[5].append(ns)

    return (xp.reshape(batch, seq, D_MODEL), xs.reshape(dec_batch, dec_seq, D_MODEL),
            *(jnp.stack(o) for o in outs))
```
